```python
import jax, jax.numpy as jnp
from jax import lax
import numpy as np

D_MODEL = 1024
BATCH = 2
SEQ = 8192
DEPTH = 1

D_MIX = D_MODEL
HG_WIDTH = D_MIX // 2
HG_DK = 128
HG_DV = 128
HG_HEADS = HG_WIDTH // HG_DV
HG_CHUNK = 64
NSA_WIDTH = D_MIX - HG_WIDTH
NSA_DH = 64
NSA_HEADS = NSA_WIDTH // NSA_DH
NSA_KV_HEADS = 2
NSA_GROUP = NSA_HEADS // NSA_KV_HEADS
CMP_LEN = 32
CMP_STRIDE = 16
SLC_LEN = 64
SLC_TOPK = 16
WIN = 512
Q_BLOCK = 128
PLE_DIM = 256
PEER_HEADS = 8
PEER_NKEYS = 128
PEER_N = PEER_NKEYS * PEER_NKEYS
PEER_DKEY = 256
PEER_TOPK = 16
PEER_TOKEN_BLOCK = 128
EPS = 1e-6

KV_WIDTH = NSA_KV_HEADS * NSA_DH
SPLIT_SIZES = (HG_WIDTH, HG_WIDTH, HG_WIDTH, HG_WIDTH, NSA_WIDTH,
               KV_WIDTH, KV_WIDTH, KV_WIDTH, KV_WIDTH, KV_WIDTH, KV_WIDTH, 3 * NSA_HEADS)
SPLIT_POINTS = tuple(int(v) for v in np.cumsum(SPLIT_SIZES)[:-1])
D_IN = int(sum(SPLIT_SIZES))

kernel_name = 'hybrid_hgrn2_nsa_peer_block'


def rmsnorm(x, g):
    x32 = x.astype(jnp.float32)
    y = x32 * lax.rsqrt(jnp.mean(x32 * x32, axis=-1, keepdims=True) + EPS)
    return (y * g.astype(jnp.float32)).astype(x.dtype)


def alibi_slopes(n_heads):
    return jnp.asarray(2.0 ** (-8.0 * np.arange(1, n_heads + 1) / n_heads), dtype=jnp.float32)


def masked_softmax(s, mask):
    s = jnp.where(mask, s, -jnp.inf)
    m = jnp.max(s, axis=-1, keepdims=True)
    m = jnp.where(jnp.isfinite(m), m, 0.0)
    e = jnp.where(mask, jnp.exp(s - m), 0.0)
    return e / jnp.maximum(jnp.sum(e, axis=-1, keepdims=True), 1e-30)


def hgrn2_mixer(q, f_logit, i_val, g, lb, out_gain):
    B, T, _ = q.shape
    f32 = jnp.float32
    C = HG_CHUNK
    n_chunk = T // C
    forget = lb + (1.0 - lb) * jax.nn.sigmoid(f_logit.astype(f32))
    log_f = jnp.log(forget)
    key = 1.0 - forget
    query = jax.nn.silu(q.astype(f32)) * HG_DK ** -0.5

    def chunks(a, d):
        return a.reshape(B, n_chunk, C, HG_HEADS, d).transpose(0, 3, 1, 2, 4)

    qc = chunks(query, HG_DK)
    kc = chunks(key, HG_DK)
    vc = chunks(i_val.astype(f32), HG_DV)
    G = jnp.cumsum(chunks(log_f, HG_DK), axis=3)
    G_ref = G[:, :, :, C // 2 - 1:C // 2]
    G_last = G[:, :, :, C - 1:C]
    causal = jnp.tril(jnp.ones((C, C), dtype=bool))
    a = jnp.einsum('bhntk,bhnsk->bhnts', qc * jnp.exp(G - G_ref), kc * jnp.exp(G_ref - G))
    o_intra = jnp.einsum('bhnts,bhnsv->bhntv', jnp.where(causal, a, 0.0), vc)
    q_in = jnp.moveaxis(qc * jnp.exp(G), 2, 0)
    k_in = jnp.moveaxis(kc * jnp.exp(G_last - G), 2, 0)
    v_in = jnp.moveaxis(vc, 2, 0)
    d_in = jnp.moveaxis(jnp.exp(G_last[:, :, :, 0]), 2, 0)

    def step(S, xs):
        q_t, k_t, v_t, d_t = xs
        o_t = jnp.einsum('bhtk,bhkv->bhtv', q_t, S)
        S = d_t[..., None] * S + jnp.einsum('bhtk,bhtv->bhkv', k_t, v_t)
        return S, o_t

    S0 = jnp.zeros((B, HG_HEADS, HG_DK, HG_DV), f32)
    _, o_inter = lax.scan(step, S0, (q_in, k_in, v_in, d_in))
    o = o_intra + jnp.moveaxis(o_inter, 0, 2)
    o = o.transpose(0, 2, 3, 1, 4).reshape(B, T, HG_HEADS, HG_DV)
    gate = jax.nn.silu(g.astype(f32)).reshape(B, T, HG_HEADS, HG_DV)
    return (rmsnorm(o, out_gain.reshape(HG_HEADS, HG_DV)) * gate).reshape(B, T, HG_WIDTH)


def cmp_to_slc(imp, n_slc):
    ratio = SLC_LEN // CMP_STRIDE
    span = CMP_LEN // CMP_STRIDE
    weights = np.convolve(np.ones(ratio), np.ones(span))
    n_cmp = imp.shape[-1]
    pad = n_slc * ratio + ratio + span - n_cmp
    padded = jnp.pad(imp, [(0, 0)] * (imp.ndim - 1) + [(0, pad)])
    out = float(weights[0]) * padded[..., 0:n_slc * ratio:ratio]
    for o in range(1, len(weights)):
        out = out + float(weights[o]) * padded[..., o:o + n_slc * ratio:ratio]
    return out


def nsa_mixer(q, k_cmp, v_cmp, k_slc, v_slc, k_win, v_win, gate_logit,
              q_gain, k_gain, cmp_pe, cmp_w1, cmp_w2, out_gain):
    B, T, _ = q.shape
    f32 = jnp.float32
    G, R = NSA_KV_HEADS, NSA_GROUP

    def heads(a, n):
        return a.astype(f32).reshape(B, T, n, NSA_DH).transpose(0, 2, 1, 3)

    qh = rmsnorm(heads(q, NSA_HEADS), q_gain) * NSA_DH ** -0.5
    qg = qh.reshape(B, G, R, T, NSA_DH)
    n_cmp = (T - CMP_LEN) // CMP_STRIDE + 1

    def compress(a, pe, w1, w2):
        c = a.reshape(B, G, T // CMP_STRIDE, CMP_STRIDE, NSA_DH)
        blk = jnp.concatenate([c[:, :, j:j + n_cmp] for j in range(CMP_LEN // CMP_STRIDE)], axis=3)
        blk = (blk + pe).reshape(B, G, n_cmp, CMP_LEN * NSA_DH)
        return jax.nn.gelu(blk @ w1) @ w2

    kc = rmsnorm(compress(heads(k_cmp, G), cmp_pe[0], cmp_w1[0], cmp_w2[0]), k_gain[0])
    vc = compress(heads(v_cmp, G), cmp_pe[1], cmp_w1[1], cmp_w2[1])
    cmp_end = jnp.arange(n_cmp) * CMP_STRIDE + CMP_LEN - 1
    n_slc = T // SLC_LEN
    top_n = min(SLC_TOPK, n_slc)
    k_sb = rmsnorm(heads(k_slc, G), k_gain[1]).reshape(B, G, n_slc, SLC_LEN, NSA_DH)
    v_sb = heads(v_slc, G).reshape(B, G, n_slc, SLC_LEN, NSA_DH)
    pad_w = ((0, 0), (0, 0), (WIN, 0), (0, 0))
    k_wp = jnp.pad(rmsnorm(heads(k_win, G), k_gain[2]), pad_w)
    v_wp = jnp.pad(heads(v_win, G), pad_w)
    gates = jax.nn.sigmoid(gate_logit.astype(f32)).reshape(B, T, NSA_HEADS, 3).transpose(0, 2, 1, 3)
    gates = gates.reshape(B, G, R, T, 3)
    slopes = alibi_slopes(NSA_HEADS).reshape(1, G, R, 1, 1)
    gather_blocks = jax.vmap(jax.vmap(lambda blocks, ix: blocks[ix]))

    def query_block(b):
        t0 = b * Q_BLOCK
        qb = lax.dynamic_slice_in_dim(qg, t0, Q_BLOCK, axis=3)
        gb = lax.dynamic_slice_in_dim(gates, t0, Q_BLOCK, axis=3)
        t = t0 + jnp.arange(Q_BLOCK)
        dist_c = t[:, None] - cmp_end[None, :]
        s_c = jnp.einsum('bgrtd,bgnd->bgrtn', qb, kc) - slopes * dist_c
        p_c = masked_softmax(s_c, dist_c >= 0)
        o_c = jnp.einsum('bgrtn,bgnd->bgrtd', p_c, vc)
        imp = cmp_to_slc(jnp.sum(p_c, axis=2), n_slc)
        blk_id = jnp.arange(n_slc)[None, :]
        cur = (t // SLC_LEN)[:, None]
        forced = (blk_id == 0) | (blk_id == cur) | (blk_id == cur - 1)
        imp = jnp.where(forced, jnp.inf, jnp.where(blk_id <= cur, imp, -jnp.inf))
        _, sel = lax.top_k(imp, top_n)
        ks = gather_blocks(k_sb, sel).reshape(B, G, Q_BLOCK, top_n * SLC_LEN, NSA_DH)
        vs = gather_blocks(v_sb, sel).reshape(B, G, Q_BLOCK, top_n * SLC_LEN, NSA_DH)
        pos = (sel[..., None] * SLC_LEN + jnp.arange(SLC_LEN)).reshape(B, G, Q_BLOCK, top_n * SLC_LEN)
        dist_s = (t[:, None] - pos)[:, :, None]
        s_s = jnp.einsum('bgrtd,bgtmd->bgrtm', qb, ks) - slopes * dist_s
        p_s = masked_softmax(s_s, dist_s >= 0)
        o_s = jnp.einsum('bgrtm,bgtmd->bgrtd', p_s, vs)
        kw = lax.dynamic_slice_in_dim(k_wp, t0, WIN + Q_BLOCK, axis=2)
        vw = lax.dynamic_slice_in_dim(v_wp, t0, WIN + Q_BLOCK, axis=2)
        pos_w = t0 - WIN + jnp.arange(WIN + Q_BLOCK)
        dist_w = t[:, None] - pos_w[None, :]
        valid_w = (dist_w >= 0) & (dist_w < WIN) & (pos_w[None, :] >= 0)
        s_w = jnp.einsum('bgrtd,bgsd->bgrts', qb, kw) - slopes * dist_w
        p_w = masked_softmax(s_w, valid_w)
        o_w = jnp.einsum('bgrts,bgsd->bgrtd', p_w, vw)
        return gb[..., 0:1] * o_c + gb[..., 1:2] * o_s + gb[..., 2:3] * o_w

    o = lax.map(query_block, jnp.arange(T // Q_BLOCK))
    o = o.transpose(1, 0, 4, 2, 3, 5).reshape(B, T, NSA_HEADS, NSA_DH)
    return rmsnorm(o, out_gain.reshape(NSA_HEADS, NSA_DH)).reshape(B, T, NSA_WIDTH)


def peer_ffn(h, wq, sub_keys, u, v):
    B, T, D = h.shape
    f32 = jnp.float32
    tokens = h.reshape(-1, PEER_TOKEN_BLOCK, D)

    def token_block(xt):
        q = (xt @ wq).astype(f32).reshape(PEER_TOKEN_BLOCK, PEER_HEADS, 2, PEER_DKEY // 2)
        s = jnp.einsum('thcd,hckd->thck', q, sub_keys.astype(f32))
        top_s, top_i = lax.top_k(s, PEER_TOPK)
        cand = top_s[:, :, 0, :, None] + top_s[:, :, 1, None, :]
        best_s, best_c = lax.top_k(cand.reshape(PEER_TOKEN_BLOCK, PEER_HEADS, PEER_TOPK * PEER_TOPK), PEER_TOPK)
        i1 = jnp.take_along_axis(top_i[:, :, 0], best_c // PEER_TOPK, axis=-1)
        i2 = jnp.take_along_axis(top_i[:, :, 1], best_c % PEER_TOPK, axis=-1)
        expert = i1 * PEER_NKEYS + i2
        w = jax.nn.softmax(best_s, axis=-1)
        act = jax.nn.gelu(jnp.einsum('thkd,td->thk', u[expert].astype(f32), xt.astype(f32)))
        return jnp.einsum('thk,thkd->td', w * act, v[expert].astype(f32))

    return lax.map(token_block, tokens).reshape(B, T, D).astype(h.dtype)


def setup_inputs(seed: int = 0) -> dict:
    key = jax.random.key(seed)
    ks = jax.random.split(key, 21)
    n = jax.random.normal
    f32 = jnp.float32

    def gain(k, shape):
        return 1.0 + 0.02 * n(k, shape, f32)

    return {
        'x': n(ks[0], (BATCH, SEQ, D_MODEL), f32),
        'p': n(ks[1], (DEPTH, BATCH, SEQ, PLE_DIM), f32),
        'mix_norm': gain(ks[2], (DEPTH, D_MODEL)),
        'w_in': n(ks[3], (DEPTH, D_MODEL, D_IN), f32) * D_MODEL ** -0.5,
        'hg_lb_logits': 0.1 * n(ks[4], (DEPTH + 1, HG_WIDTH), f32),
        'hg_out_norm': gain(ks[5], (DEPTH, HG_WIDTH)),
        'nsa_q_norm': gain(ks[6], (DEPTH, NSA_DH)),
        'nsa_k_norm': gain(ks[7], (DEPTH, 3, NSA_DH)),
        'cmp_pe': 0.02 * n(ks[8], (DEPTH, 2, CMP_LEN, NSA_DH), f32),
        'cmp_w1': n(ks[9], (DEPTH, 2, CMP_LEN * NSA_DH, NSA_DH), f32) * (CMP_LEN * NSA_DH) ** -0.5,
        'cmp_w2': n(ks[10], (DEPTH, 2, NSA_DH, NSA_DH), f32) * NSA_DH ** -0.5,
        'nsa_out_norm': gain(ks[11], (DEPTH, NSA_WIDTH)),
        'w_out': n(ks[12], (DEPTH, D_MIX, D_MODEL), f32) * D_MIX ** -0.5,
        'ffn_norm': gain(ks[13], (DEPTH, D_MODEL)),
        'peer_wq': n(ks[14], (DEPTH, D_MODEL, PEER_HEADS * PEER_DKEY), f32) * D_MODEL ** -0.5,
        'peer_keys': n(ks[15], (DEPTH, PEER_HEADS, 2, PEER_NKEYS, PEER_DKEY // 2), f32) * (PEER_DKEY // 2) ** -0.5,
        'peer_u': n(ks[16], (DEPTH, PEER_N, D_MODEL), f32) * D_MODEL ** -0.5,
        'peer_v': n(ks[17], (DEPTH, PEER_N, D_MODEL), f32) * PEER_HEADS ** -0.5,
        'ple_proj': n(ks[18], (DEPTH, PLE_DIM, D_MODEL), f32) * PLE_DIM ** -0.5,
        'ple_gate_norm': gain(ks[19], (DEPTH, D_MODEL)),
        'ple_gate_w': n(ks[20], (DEPTH, D_MODEL, D_MODEL), f32) * D_MODEL ** -0.5,
    }


def reference(x, p, mix_norm, w_in, hg_lb_logits, hg_out_norm, nsa_q_norm, nsa_k_norm,
              cmp_pe, cmp_w1, cmp_w2, nsa_out_norm, w_out, ffn_norm, peer_wq, peer_keys,
              peer_u, peer_v, ple_proj, ple_gate_norm, ple_gate_w):
    lb_table = jnp.cumsum(jax.nn.softmax(hg_lb_logits.astype(jnp.float32), axis=0), axis=0)
    h = x
    for i in range(DEPTH):
        a = rmsnorm(h, mix_norm[i])
        proj = a @ w_in[i]
        (hg_q, hg_f, hg_i, hg_g, n_q, n_kc, n_vc, n_ks, n_vs, n_kw, n_vw, n_gate) = jnp.split(proj, SPLIT_POINTS, axis=-1)
        o_hg = hgrn2_mixer(hg_q, hg_f, hg_i, hg_g, lb_table[i], hg_out_norm[i])
        o_nsa = nsa_mixer(n_q, n_kc, n_vc, n_ks, n_vs, n_kw, n_vw, n_gate,
                          nsa_q_norm[i], nsa_k_norm[i], cmp_pe[i], cmp_w1[i], cmp_w2[i], nsa_out_norm[i])
        mixed = jnp.concatenate([o_hg, o_nsa], axis=-1).astype(h.dtype)
        h = h + mixed @ w_out[i]
        h = h + peer_ffn(rmsnorm(h, ffn_norm[i]), peer_wq[i], peer_keys[i], peer_u[i], peer_v[i])
        gate = jax.nn.sigmoid(rmsnorm(h, ple_gate_norm[i]) @ ple_gate_w[i])
        h = h + (p[i] @ ple_proj[i]) * gate
    return h
```

```python
import functools

import numpy as np
import jax
import jax.numpy as jnp
from jax import lax
from jax.experimental import pallas as pl
from jax.experimental.pallas import tpu as pltpu

F32 = jnp.float32
BF16 = jnp.bfloat16
HIGHEST = lax.Precision.HIGHEST
EPS = 1e-6
NEG = -1e30
REMOVED = -3e38

HG_HEADS = 4
HG_D = 128
HG_CHUNK = 64
HG_WIDTH = HG_HEADS * HG_D
NSA_DH = 64
NSA_HEADS = 8
NSA_G = 2
NSA_R = NSA_HEADS // NSA_G
CMP_LEN = 32
CMP_STRIDE = 16
SLC_LEN = 64
SLC_TOPK = 16
WIN = 512
TQ = 128
PEER_HEADS = 8
PEER_NKEYS = 128
PEER_TOPK = 16
VMEM_LIMIT = 56 * 1024 * 1024


def _dot(a, b, precision=None):
    return lax.dot_general(a, b, (((1,), (0,)), ((), ())), preferred_element_type=F32, precision=precision)


def _dot_nt(a, b):
    return lax.dot_general(a, b, (((1,), (1,)), ((), ())), preferred_element_type=F32)


def _dot_tn(a, b):
    return lax.dot_general(a, b, (((0,), (0,)), ((), ())), preferred_element_type=F32)


def _rms(x):
    return x * lax.rsqrt(jnp.mean(x * x, axis=-1, keepdims=True) + EPS)


def _params(*sem):
    return pltpu.CompilerParams(dimension_semantics=sem, vmem_limit_bytes=VMEM_LIMIT)


def _inproj_kernel(x_ref, g_ref, whg_ref, wq_ref, wkv_ref, wgate_ref, qg_ref, kg_ref,
                   hg_ref, qn_ref, kcr_ref, vcr_ref, ks_ref, vs_ref, kw_ref, vw_ref, gate_ref):
    a = (_rms(x_ref[...]) * g_ref[...]).astype(BF16)
    hg_ref[...] = _dot(a, whg_ref[...])
    q = _dot(a, wq_ref[...])
    for h in range(NSA_HEADS):
        qh = q[:, NSA_DH * h:NSA_DH * (h + 1)]
        qn_ref[0, h] = (_rms(qh) * qg_ref[...] * NSA_DH ** -0.5).astype(BF16)
    kv = _dot(a, wkv_ref[...])

    def part(j, g):
        return kv[:, 128 * j + NSA_DH * g:128 * j + NSA_DH * (g + 1)]

    for g in range(NSA_G):
        kcr_ref[0, g] = part(0, g)
        vcr_ref[0, g] = part(1, g)
        ks_ref[0, g] = (_rms(part(2, g)) * kg_ref[1:2, :]).astype(BF16)
        vs_ref[0, g] = part(3, g).astype(BF16)
        kw_ref[0, g] = (_rms(part(4, g)) * kg_ref[2:3, :]).astype(BF16)
        vw_ref[0, g] = part(5, g).astype(BF16)
    gate_ref[...] = jax.nn.sigmoid(_dot(a, wgate_ref[...]))


def _inproj(x2, g, whg, wq, wkv, wgate, qg, kg, B, T, tb):
    N, D = x2.shape
    nt = T // tb
    full = lambda shape: pl.BlockSpec(shape, lambda i: (0,) * len(shape))
    head_spec = lambda n: pl.BlockSpec((1, n, tb, NSA_DH), lambda i: (i // nt, 0, i % nt, 0))
    kv_shape = lambda dt: jax.ShapeDtypeStruct((B, NSA_G, T, NSA_DH), dt)
    return pl.pallas_call(
        _inproj_kernel,
        grid=(N // tb,),
        in_specs=[pl.BlockSpec((tb, D), lambda i: (i, 0)), full(g.shape), full(whg.shape), full(wq.shape),
                  full(wkv.shape), full(wgate.shape), full(qg.shape), full(kg.shape)],
        out_specs=[pl.BlockSpec((tb, 4 * HG_WIDTH), lambda i: (i, 0)), head_spec(NSA_HEADS),
                   head_spec(NSA_G), head_spec(NSA_G), head_spec(NSA_G), head_spec(NSA_G),
                   head_spec(NSA_G), head_spec(NSA_G), pl.BlockSpec((tb, 256), lambda i: (i, 0))],
        out_shape=[jax.ShapeDtypeStruct((N, 4 * HG_WIDTH), F32),
                   jax.ShapeDtypeStruct((B, NSA_HEADS, T, NSA_DH), BF16),
                   kv_shape(F32), kv_shape(F32), kv_shape(BF16), kv_shape(BF16), kv_shape(BF16), kv_shape(BF16),
                   jax.ShapeDtypeStruct((N, 256), F32)],
        compiler_params=_params("parallel"),
        name="inproj",
    )(x2, g, whg, wq, wkv, wgate, qg, kg)


def _hgrn_kernel(hg_ref, lbl_ref, gain_ref, o_ref, st_ref, *, nb, cpb):
    @pl.when(pl.program_id(0) == 0)
    def _():
        st_ref[...] = jnp.zeros_like(st_ref)

    logits = lbl_ref[...]
    ex = jnp.exp(logits - jnp.max(logits, axis=0, keepdims=True))
    lb_all = ex[0:1, :] / jnp.sum(ex, axis=0, keepdims=True)
    C = HG_CHUNK
    row = lax.broadcasted_iota(jnp.int32, (C, C), 0)
    col = lax.broadcasted_iota(jnp.int32, (C, C), 1)
    tril = row >= col
    trilf = jnp.where(tril, 1.0, 0.0).astype(F32)

    def chunk(c, carry):
        r0 = pl.multiple_of(c * C, C)
        rows = pl.ds(r0, C)
        for b in range(nb):
            for h in range(HG_HEADS):
                sl = slice(HG_D * h, HG_D * (h + 1))
                q = hg_ref[b, rows, HG_D * h:HG_D * (h + 1)]
                f = hg_ref[b, rows, HG_WIDTH + HG_D * h:HG_WIDTH + HG_D * (h + 1)]
                iv = hg_ref[b, rows, 2 * HG_WIDTH + HG_D * h:2 * HG_WIDTH + HG_D * (h + 1)]
                g = hg_ref[b, rows, 3 * HG_WIDTH + HG_D * h:3 * HG_WIDTH + HG_D * (h + 1)]
                lb = lb_all[:, sl]
                forget = lb + (1.0 - lb) * jax.nn.sigmoid(f)
                logf = jnp.log(forget)
                key = 1.0 - forget
                query = jax.nn.silu(q) * HG_D ** -0.5
                G = _dot(trilf, logf, HIGHEST)
                g_ref = G[C // 2 - 1:C // 2, :]
                g_last = G[C - 1:C, :]
                qa = (query * jnp.exp(G - g_ref)).astype(BF16)
                ka = (key * jnp.exp(g_ref - G)).astype(BF16)
                a = jnp.where(tril, _dot_nt(qa, ka), 0.0)
                vb = iv.astype(BF16)
                o = _dot(a.astype(BF16), vb)
                st = st_ref[b * HG_HEADS + h]
                o = o + _dot_nt((query * jnp.exp(G)).astype(BF16), st.astype(BF16))
                kb = (key * jnp.exp(g_last - G)).astype(BF16)
                st_ref[b * HG_HEADS + h] = st * jnp.exp(g_last) + _dot_tn(vb, kb)
                on = _rms(o) * gain_ref[:, sl] * jax.nn.silu(g)
                o_ref[b, rows, sl] = on.astype(o_ref.dtype)
        return carry

    lax.fori_loop(0, cpb, chunk, 0)


def _hgrn(hg3, lb_logits, gain, cpb):
    B, T, W = hg3.shape
    rows = cpb * HG_CHUNK
    return pl.pallas_call(
        functools.partial(_hgrn_kernel, nb=B, cpb=cpb),
        grid=(T // rows,),
        in_specs=[pl.BlockSpec((B, rows, W), lambda i: (0, i, 0)),
                  pl.BlockSpec(lb_logits.shape, lambda i: (0, 0)),
                  pl.BlockSpec(gain.shape, lambda i: (0, 0))],
        out_specs=pl.BlockSpec((B, rows, HG_WIDTH), lambda i: (0, i, 0)),
        out_shape=jax.ShapeDtypeStruct((B, T, HG_WIDTH), BF16),
        scratch_shapes=[pltpu.VMEM((B * HG_HEADS, HG_D, HG_D), F32)],
        compiler_params=_params("arbitrary"),
        name="hgrn2",
    )(hg3, lb_logits, gain)


def _cmp_kernel(xk_ref, xv_ref, w1_ref, w2_ref, pe_ref, kg_ref, kc_ref, vc_ref):
    half = CMP_STRIDE * NSA_DH
    for idx, (x_ref, o_ref) in enumerate(((xk_ref, kc_ref), (xv_ref, vc_ref))):
        x = x_ref[0]
        n = x.shape[0]
        w1 = w1_ref[idx]
        first = _dot(x, w1[:half], HIGHEST)
        second = _dot(x, w1[half:], HIGHEST)
        const = _dot(pe_ref[idx], w1, HIGHEST)[0:1, :]
        y = jax.nn.gelu(first + pltpu.roll(second, n - 1, 0) + const)
        y = _dot(y, w2_ref[idx], HIGHEST)
        if idx == 0:
            y = _rms(y) * kg_ref[0:1, :]
        o_ref[0] = y.astype(BF16)


def _compress(xk, xv, w1, w2, pe8, kg):
    BG, n, K = xk.shape
    full = lambda a: pl.BlockSpec(a.shape, lambda i: (0,) * a.ndim)
    blk = pl.BlockSpec((1, n, K), lambda i: (i, 0, 0))
    oblk = pl.BlockSpec((1, n, NSA_DH), lambda i: (i, 0, 0))
    return pl.pallas_call(
        _cmp_kernel,
        grid=(BG,),
        in_specs=[blk, blk, full(w1), full(w2), full(pe8), full(kg)],
        out_specs=[oblk, oblk],
        out_shape=[jax.ShapeDtypeStruct((BG, n, NSA_DH), BF16)] * 2,
        compiler_params=_params("parallel"),
        name="nsa_compress",
    )(xk, xv, w1, w2, pe8, kg)


def _nsa_kernel(q_ref, gate_ref, kc_ref, vc_ref, ks_ref, vs_ref, kw_ref, vw_ref, mconv_ref, og_ref,
                o_ref, m_s, l_s, acc_s, sel_s):
    g = pl.program_id(1)
    i = pl.program_id(2)
    t0 = i * TQ
    q = q_ref[0].reshape(NSA_R * TQ, NSA_DH)
    slopes = [jnp.where(g == 0, 2.0 ** -(h + 1), 2.0 ** -(NSA_R + h + 1)).astype(F32) for h in range(NSA_R)]
    hrows = [slice(h * TQ, (h + 1) * TQ) for h in range(NSA_R)]

    ncp = kc_ref.shape[2]
    rr = lax.broadcasted_iota(jnp.int32, (TQ, ncp), 0)
    cc = lax.broadcasted_iota(jnp.int32, (TQ, ncp), 1)
    dist_c = t0 + rr - (cc * CMP_STRIDE + CMP_LEN - 1)
    valid_c = dist_c >= 0
    dist_cf = dist_c.astype(F32)
    s_c = _dot_nt(q, kc_ref[0, 0])
    vc = vc_ref[0, 0]
    psum = jnp.zeros((TQ, ncp), F32)
    o_cmp = []
    for h in range(NSA_R):
        s = jnp.where(valid_c, s_c[hrows[h]] - slopes[h] * dist_cf, NEG)
        m = jnp.max(s, axis=-1, keepdims=True)
        m = jnp.where(m > 0.1 * NEG, m, 0.0)
        e = jnp.where(valid_c, jnp.exp(s - m), 0.0)
        p = e / jnp.maximum(jnp.sum(e, axis=-1, keepdims=True), 1e-30)
        psum = psum + p
        o_cmp.append(_dot(p.astype(BF16), vc))

    imp = _dot(psum, mconv_ref[...], HIGHEST)
    blk = lax.broadcasted_iota(jnp.int32, (TQ, 128), 1).astype(F32)
    r128 = lax.broadcasted_iota(jnp.int32, (TQ, 128), 0)
    cur = jnp.right_shift(t0 + r128, 6).astype(F32)
    forced = (blk == 0.0) | (blk == cur) | (blk == cur - 1.0)
    v = jnp.where(forced, -NEG, jnp.where(blk <= cur, imp, NEG))
    sel = jnp.zeros((TQ, 128), F32)
    for _ in range(SLC_TOPK):
        m = jnp.max(v, axis=-1, keepdims=True)
        idx = jnp.min(jnp.where(v == m, blk, 128.0), axis=-1, keepdims=True)
        pick = blk == idx
        sel = jnp.where(pick, 1.0, sel)
        v = jnp.where(pick, REMOVED, v)
    sel_s[...] = sel.astype(BF16)

    m_s[...] = jnp.full(m_s.shape, NEG, F32)
    l_s[...] = jnp.zeros(l_s.shape, F32)
    acc_s[...] = jnp.zeros(acc_s.shape, F32)
    er = lax.broadcasted_iota(jnp.int32, (TQ, TQ), 0)
    ec = lax.broadcasted_iota(jnp.int32, (TQ, TQ), 1)
    lane = lax.broadcasted_iota(jnp.int32, (1, TQ), 1)

    def flash(br, k, vv, negm, k0):
        s_all = _dot_nt(q, k)
        rel = (k0 + lane - t0).astype(F32)
        for h in range(NSA_R):
            rows = hrows[h]
            s = s_all[rows] + slopes[h] * rel + negm
            m_old = m_s[br, rows]
            m_new = jnp.maximum(m_old, jnp.max(s, axis=-1, keepdims=True))
            alpha = jnp.exp(m_old - m_new)
            p = jnp.exp(s - m_new)
            l_s[br, rows] = alpha * l_s[br, rows] + jnp.sum(p, axis=-1, keepdims=True)
            acc_s[br, rows] = alpha * acc_s[br, rows] + _dot(p.astype(BF16), vv)
            m_s[br, rows] = m_new

    def slc_body(c, carry):
        k0 = pl.multiple_of(c * TQ, TQ)
        expand = jnp.where(er == 2 * c + jnp.where(ec >= SLC_LEN, 1, 0), 1.0, 0.0).astype(BF16)
        selx = _dot(sel_s[...], expand)
        negm = jnp.where(selx > 0.5, jnp.where(k0 + ec <= t0 + er, 0.0, NEG), NEG)
        flash(0, ks_ref[0, 0, pl.ds(k0, TQ), :], vs_ref[0, 0, pl.ds(k0, TQ), :], negm, k0)
        return carry

    lax.fori_loop(0, i + 1, slc_body, 0)

    def win_chunk(c):
        k0 = pl.multiple_of(c * TQ, TQ)
        dist = t0 + er - (k0 + ec)
        negm = jnp.where(dist >= 0, jnp.where(dist < WIN, 0.0, NEG), NEG)
        flash(1, kw_ref[0, 0, pl.ds(k0, TQ), :], vw_ref[0, 0, pl.ds(k0, TQ), :], negm, k0)

    win_chunk(i)
    for back in range(1, WIN // TQ + 1):
        @pl.when(i >= back)
        def _(back=back):
            win_chunk(i - back)

    gts = gate_ref[...]
    for h in range(NSA_R):
        rows = hrows[h]
        o_s = acc_s[0, rows] / l_s[0, rows]
        o_w = acc_s[1, rows] / l_s[1, rows]
        o = gts[:, 3 * h:3 * h + 1] * o_cmp[h] + gts[:, 3 * h + 1:3 * h + 2] * o_s + gts[:, 3 * h + 2:3 * h + 3] * o_w
        o_ref[:, NSA_DH * h:NSA_DH * (h + 1)] = (_rms(o) * og_ref[:, NSA_DH * h:NSA_DH * (h + 1)]).astype(o_ref.dtype)


def _nsa(qn, gates, kc, vc, ks, vs, kw, vw, mconv, og):
    B, _, T, _ = qn.shape
    nq = T // TQ
    ncp = kc.shape[2]
    kvspec = pl.BlockSpec((1, 1, T, NSA_DH), lambda b, g, i: (b, g, 0, 0))
    cspec = pl.BlockSpec((1, 1, ncp, NSA_DH), lambda b, g, i: (b, g, 0, 0))
    width = NSA_R * NSA_DH
    return pl.pallas_call(
        _nsa_kernel,
        grid=(B, NSA_G, nq),
        in_specs=[pl.BlockSpec((1, NSA_R, TQ, NSA_DH), lambda b, g, i: (b, g, i, 0)),
                  pl.BlockSpec((TQ, 128), lambda b, g, i: (b * nq + i, g)),
                  cspec, cspec, kvspec, kvspec, kvspec, kvspec,
                  pl.BlockSpec(mconv.shape, lambda b, g, i: (0, 0)),
                  pl.BlockSpec((1, width), lambda b, g, i: (0, g))],
        out_specs=pl.BlockSpec((TQ, width), lambda b, g, i: (b * nq + i, g)),
        out_shape=jax.ShapeDtypeStruct((B * T, NSA_G * width), BF16),
        scratch_shapes=[pltpu.VMEM((2, NSA_R * TQ, 1), F32), pltpu.VMEM((2, NSA_R * TQ, 1), F32),
                        pltpu.VMEM((2, NSA_R * TQ, NSA_DH), F32), pltpu.VMEM((TQ, 128), BF16)],
        compiler_params=_params("parallel", "parallel", "arbitrary"),
        name="nsa_attn",
    )(qn, gates, kc, vc, ks, vs, kw, vw, mconv, og)


def _peer_sel_kernel(x_ref, ohg_ref, onsa_ref, wo1_ref, wo2_ref, fg_ref, wqt_ref, keys_ref,
                     h1_ref, xnt_ref, rank2_ref, e2_ref, lim1_ref, e1_ref, qt_s):
    h1 = x_ref[...] + _dot(ohg_ref[...], wo1_ref[...]) + _dot(onsa_ref[...], wo2_ref[...])
    h1_ref[...] = h1
    xnt = (_rms(h1) * fg_ref[...]).T.astype(BF16)
    xnt_ref[...] = xnt
    qt_s[...] = _dot(wqt_ref[...], xnt)
    tb = xnt.shape[1]
    K = PEER_TOPK
    rowi = lax.broadcasted_iota(jnp.int32, (PEER_NKEYS, tb), 0).astype(F32)
    ci = lax.broadcasted_iota(jnp.int32, (K * K, tb), 0).astype(F32)
    k1i = lax.broadcasted_iota(jnp.int32, (K, tb), 0).astype(F32)

    def topk(s):
        rank = jnp.full(s.shape, float(PEER_NKEYS), F32)
        tops = []
        for k in range(K):
            m = jnp.max(s, axis=0, keepdims=True)
            idx = jnp.min(jnp.where(s == m, rowi, float(PEER_NKEYS)), axis=0, keepdims=True)
            pick = rowi == idx
            rank = jnp.where(pick, float(k), rank)
            s = jnp.where(pick, REMOVED, s)
            tops.append(m)
        return rank, jnp.concatenate(tops, axis=0)

    def head(h, carry):
        r = pl.multiple_of(h * 2 * PEER_NKEYS, 2 * PEER_NKEYS)
        s1 = _dot(keys_ref[2 * h], qt_s[pl.ds(r, PEER_NKEYS), :].astype(BF16))
        s2 = _dot(keys_ref[2 * h + 1], qt_s[pl.ds(r + PEER_NKEYS, PEER_NKEYS), :].astype(BF16))
        r1, t1 = topk(s1)
        r2, t2 = topk(s2)
        cand = jnp.concatenate([t1[k:k + 1, :] + t2 for k in range(K)], axis=0)
        m0 = t1[0:1, :] + t2[0:1, :]
        lim = jnp.zeros((K, tb), F32)
        z = jnp.zeros((1, tb), F32)
        for k in range(K):
            m = jnp.max(cand, axis=0, keepdims=True)
            idx = jnp.min(jnp.where(cand == m, ci, float(K * K)), axis=0, keepdims=True)
            cand = jnp.where(ci == idx, REMOVED, cand)
            lim = lim + jnp.where(k1i == jnp.floor(idx * (1.0 / K)), 1.0, 0.0)
            z = z + jnp.exp(m - m0)
        lim1 = jnp.zeros((PEER_NKEYS, tb), F32)
        for k in range(K):
            lim1 = lim1 + jnp.where(r1 == float(k), lim[k:k + 1, :], 0.0)
        rank2_ref[h] = r2
        lim1_ref[h] = lim1
        e1_ref[h] = jnp.exp(s1 - t1[0:1, :])
        e2_ref[h] = jnp.exp(s2 - t2[0:1, :]) / z
        return carry

    lax.fori_loop(0, PEER_HEADS, head, 0)


def _peer_select(x2, ohg, onsa, wo1, wo2, fg, wqt, keys, tb):
    N, D = x2.shape
    full = lambda a: pl.BlockSpec(a.shape, lambda i: (0,) * a.ndim)
    sel_spec = pl.BlockSpec((PEER_HEADS, PEER_NKEYS, tb), lambda i: (0, 0, i))
    sel_shape = jax.ShapeDtypeStruct((PEER_HEADS, PEER_NKEYS, N), F32)
    return pl.pallas_call(
        _peer_sel_kernel,
        grid=(N // tb,),
        in_specs=[pl.BlockSpec((tb, D), lambda i: (i, 0)), pl.BlockSpec((tb, ohg.shape[1]), lambda i: (i, 0)),
                  pl.BlockSpec((tb, onsa.shape[1]), lambda i: (i, 0)),
                  full(wo1), full(wo2), full(fg), full(wqt), full(keys)],
        out_specs=[pl.BlockSpec((tb, D), lambda i: (i, 0)), pl.BlockSpec((D, tb), lambda i: (0, i)),
                   sel_spec, sel_spec, sel_spec, sel_spec],
        out_shape=[jax.ShapeDtypeStruct((N, D), F32), jax.ShapeDtypeStruct((D, N), BF16),
                   sel_shape, sel_shape, sel_shape, sel_shape],
        scratch_shapes=[pltpu.VMEM((wqt.shape[0], tb), F32)],
        compiler_params=_params("parallel"),
        name="peer_select",
    )(x2, ohg, onsa, wo1, wo2, fg, wqt, keys)


def _peer_dense_kernel(xnt_ref, u_ref, vt_ref, rank2_ref, e2_ref, lim1_ref, e1_ref, h1_ref, p_ref,
                       pproj_ref, pgn_ref, pgw_ref, out_ref, acc_ref, *, eb):
    e = pl.program_id(1)

    @pl.when(e == 0)
    def _():
        acc_ref[...] = jnp.zeros_like(acc_ref)

    act = jax.nn.gelu(_dot(u_ref[...], xnt_ref[...]))
    groups = eb // PEER_NKEYS
    parts = []
    for j in range(groups):
        i1 = e * groups + j
        wt = None
        for h in range(PEER_HEADS):
            lim = lim1_ref[h, pl.ds(i1, 1), :]
            w = jnp.where(rank2_ref[h] < lim, e2_ref[h], 0.0) * e1_ref[h, pl.ds(i1, 1), :]
            wt = w if wt is None else wt + w
        parts.append((wt * act[PEER_NKEYS * j:PEER_NKEYS * (j + 1)]).astype(BF16))
    acc_ref[...] += _dot(vt_ref[...], jnp.concatenate(parts, axis=0))

    @pl.when(e == pl.num_programs(1) - 1)
    def _():
        h2 = h1_ref[...] + acc_ref[...].T
        gate = jax.nn.sigmoid(_dot((_rms(h2) * pgn_ref[...]).astype(BF16), pgw_ref[...]))
        out_ref[...] = h2 + _dot(p_ref[...].astype(BF16), pproj_ref[...]) * gate


def _peer_dense(xnt, u, vt, rank2, e2, lim1, e1, h1, p2, pproj, pgn, pgw, tb, eb):
    D, N = xnt.shape
    ne = u.shape[0] // eb
    full = lambda a: pl.BlockSpec(a.shape, lambda t, e: (0,) * a.ndim)
    sel_spec = pl.BlockSpec((PEER_HEADS, PEER_NKEYS, tb), lambda t, e: (0, 0, t))
    return pl.pallas_call(
        functools.partial(_peer_dense_kernel, eb=eb),
        grid=(N // tb, ne),
        in_specs=[pl.BlockSpec((D, tb), lambda t, e: (0, t)), pl.BlockSpec((eb, D), lambda t, e: (e, 0)),
                  pl.BlockSpec((D, eb), lambda t, e: (0, e)), sel_spec, sel_spec, sel_spec, sel_spec,
                  pl.BlockSpec((tb, D), lambda t, e: (t, 0)), pl.BlockSpec((tb, p2.shape[1]), lambda t, e: (t, 0)),
                  full(pproj), full(pgn), full(pgw)],
        out_specs=pl.BlockSpec((tb, D), lambda t, e: (t, 0)),
        out_shape=jax.ShapeDtypeStruct((N, D), F32),
        scratch_shapes=[pltpu.VMEM((D, tb), F32)],
        compiler_params=_params("parallel", "arbitrary"),
        name="peer_dense",
    )(xnt, u, vt, rank2, e2, lim1, e1, h1, p2, pproj, pgn, pgw)


def _conv_matrix(ncp):
    ratio = SLC_LEN // CMP_STRIDE
    weights = np.convolve(np.ones(ratio), np.ones(CMP_LEN // CMP_STRIDE))
    m = np.zeros((ncp, 128), np.float32)
    for j in range(128):
        for o, w in enumerate(weights):
            if ratio * j + o < ncp:
                m[ratio * j + o, j] = w
    return jnp.asarray(m)


def _block(n, pref):
    while n % pref:
        pref //= 2
    return pref


def kernel(x, p, mix_norm, w_in, hg_lb_logits, hg_out_norm, nsa_q_norm, nsa_k_norm, cmp_pe, cmp_w1, cmp_w2,
           nsa_out_norm, w_out, ffn_norm, peer_wq, peer_keys, peer_u, peer_v, ple_proj, ple_gate_norm, ple_gate_w):
    B, T, D = x.shape
    N = B * T
    x2 = x.reshape(N, D)
    nsa_w = NSA_HEADS * NSA_DH
    kvw = NSA_G * NSA_DH

    w = w_in[0]
    c0 = 4 * HG_WIDTH
    whg = w[:, :c0].astype(BF16)
    wq = w[:, c0:c0 + nsa_w].astype(BF16)
    wkv = w[:, c0 + nsa_w:c0 + nsa_w + 6 * kvw].astype(BF16)
    wg = w[:, c0 + nsa_w + 6 * kvw:]
    per_g = 3 * NSA_R
    wgate = jnp.zeros((D, 256), F32)
    for g in range(NSA_G):
        wgate = wgate.at[:, 128 * g:128 * g + per_g].set(wg[:, per_g * g:per_g * (g + 1)])
    wgate = wgate.astype(BF16)

    hg, qn, kcr, vcr, ks, vs, kw, vw, gates = _inproj(
        x2, mix_norm, whg, wq, wkv, wgate, nsa_q_norm, nsa_k_norm[0], B, T, _block(T, 256))

    o_hg = _hgrn(hg.reshape(B, T, 4 * HG_WIDTH), hg_lb_logits, hg_out_norm, _block(T // HG_CHUNK, 4))

    ncp = T // CMP_STRIDE
    stride_rows = lambda a: a.reshape(B * NSA_G, ncp, CMP_STRIDE * NSA_DH)
    pe8 = jnp.broadcast_to(cmp_pe[0].reshape(2, 1, CMP_LEN * NSA_DH), (2, 8, CMP_LEN * NSA_DH))
    kc, vc = _compress(stride_rows(kcr), stride_rows(vcr), cmp_w1[0], cmp_w2[0], pe8, nsa_k_norm[0])
    kc = kc.reshape(B, NSA_G, ncp, NSA_DH)
    vc = vc.reshape(B, NSA_G, ncp, NSA_DH)

    o_nsa = _nsa(qn, gates, kc, vc, ks, vs, kw, vw, _conv_matrix(ncp), nsa_out_norm)

    wo = w_out[0].astype(BF16)
    wqt = peer_wq[0].T.astype(BF16)
    keys = peer_keys[0].reshape(2 * PEER_HEADS, PEER_NKEYS, -1).astype(BF16)
    h1, xnt, rank2, e2, lim1, e1 = _peer_select(
        x2, o_hg.reshape(N, HG_WIDTH), o_nsa, wo[:HG_WIDTH], wo[HG_WIDTH:], ffn_norm, wqt, keys, _block(N, 256))

    out = _peer_dense(xnt, peer_u[0].astype(BF16), peer_v[0].T.astype(BF16), rank2, e2, lim1, e1, h1,
                      p[0].reshape(N, -1), ple_proj[0].astype(BF16), ple_gate_norm, ple_gate_w[0].astype(BF16),
                      _block(N, 512), 512)
    return out.reshape(B, T, D)
```

```python
import functools

import numpy as np
import jax
import jax.numpy as jnp
from jax import lax
from jax.experimental import pallas as pl
from jax.experimental.pallas import tpu as pltpu

F32 = jnp.float32
BF16 = jnp.bfloat16
HIGHEST = lax.Precision.HIGHEST
EPS = 1e-6
NEG = -1e30
REMOVED = -3e38

HG_HEADS = 4
HG_D = 128
HG_CHUNK = 64
HG_WIDTH = HG_HEADS * HG_D
NSA_DH = 64
NSA_HEADS = 8
NSA_G = 2
NSA_R = NSA_HEADS // NSA_G
CMP_LEN = 32
CMP_STRIDE = 16
SLC_LEN = 64
SLC_TOPK = 16
WIN = 512
TQ = 128
SLC_CHUNK = 512
PEER_HEADS = 8
PEER_NKEYS = 128
PEER_TOPK = 16
VMEM_LIMIT = 56 * 1024 * 1024


def _dot(a, b, precision=None):
    return lax.dot_general(a, b, (((1,), (0,)), ((), ())), preferred_element_type=F32, precision=precision)


def _dot_nt(a, b):
    return lax.dot_general(a, b, (((1,), (1,)), ((), ())), preferred_element_type=F32)


def _dot_tn(a, b):
    return lax.dot_general(a, b, (((0,), (0,)), ((), ())), preferred_element_type=F32)


def _rms(x):
    return x * lax.rsqrt(jnp.mean(x * x, axis=-1, keepdims=True) + EPS)


def _params(*sem):
    return pltpu.CompilerParams(dimension_semantics=sem, vmem_limit_bytes=VMEM_LIMIT)


def _alibi_slope_row(g, width):
    lane_h = jnp.right_shift(lax.broadcasted_iota(jnp.int32, (1, width), 1), TQ.bit_length() - 1)
    row = jnp.zeros((1, width), F32)
    for h in range(NSA_R):
        row = jnp.where(lane_h == h, jnp.where(g == 0, 2.0 ** -(h + 1), 2.0 ** -(NSA_R + h + 1)).astype(F32), row)
    return row


def _inproj_kernel(x_ref, g_ref, whg_ref, wq_ref, wkv_ref, wgate_ref, qgc_ref, kg_ref,
                   hg_ref, qt_ref, kcr_ref, vcr_ref, ks_ref, vst_ref, kw_ref, vwt_ref, gate_ref, *, nt):
    a = (_rms(x_ref[...]) * g_ref[...]).astype(BF16)
    hg_ref[...] = _dot(a, whg_ref[...])
    tb = a.shape[0]
    nqb = tb // TQ
    width = NSA_R * TQ

    qt = _dot(a, wq_ref[...]).T
    aug_row = lax.broadcasted_iota(jnp.int32, (NSA_DH, width), 0)
    for g in range(NSA_G):
        slope = _alibi_slope_row(g, width)
        aug = jnp.where(aug_row == 0, slope, jnp.where(aug_row == 1, slope * TQ, 0.0)).astype(BF16)
        for j in range(nqb):
            qt_ref[0, g, j, NSA_DH:2 * NSA_DH, :] = aug
        for h in range(NSA_R):
            blk = qt[NSA_DH * (NSA_R * g + h):NSA_DH * (NSA_R * g + h + 1)]
            r = lax.rsqrt(jnp.mean(blk * blk, axis=0, keepdims=True) + EPS)
            qn = (blk * r * qgc_ref[...] * NSA_DH ** -0.5).astype(BF16)
            for j in range(nqb):
                qt_ref[0, g, j, 0:NSA_DH, TQ * h:TQ * (h + 1)] = qn[:, TQ * j:TQ * (j + 1)]

    kv = _dot(a, wkv_ref[...])
    kvt = kv.T
    pos = (pl.program_id(0) % nt) * tb + lax.broadcasted_iota(jnp.int32, (tb, NSA_DH), 0)
    aug_col = lax.broadcasted_iota(jnp.int32, (tb, NSA_DH), 1)
    pos_aug = jnp.where(aug_col == 0, jnp.bitwise_and(pos, TQ - 1),
                        jnp.where(aug_col == 1, jnp.right_shift(pos, TQ.bit_length() - 1), 0)).astype(F32)

    def part(j, g):
        return kv[:, 128 * j + NSA_DH * g:128 * j + NSA_DH * (g + 1)]

    def part_t(j, g):
        return kvt[128 * j + NSA_DH * g:128 * j + NSA_DH * (g + 1)]

    for g in range(NSA_G):
        kcr_ref[0, g] = part(0, g)
        vcr_ref[0, g] = part(1, g)
        ks_ref[0, g] = jnp.concatenate([_rms(part(2, g)) * kg_ref[1:2, :], pos_aug], axis=1).astype(BF16)
        vst_ref[0, g] = part_t(3, g).astype(BF16)
        kw_ref[0, g] = jnp.concatenate([_rms(part(4, g)) * kg_ref[2:3, :], pos_aug], axis=1).astype(BF16)
        vwt_ref[0, g] = part_t(5, g).astype(BF16)
    gate_ref[...] = jax.nn.sigmoid(_dot(a, wgate_ref[...])).T


def _inproj(x2, g, whg, wq, wkv, wgate, qgc, kg, B, T, tb):
    N, D = x2.shape
    nt = T // tb
    nqb = tb // TQ
    full = lambda shape: pl.BlockSpec(shape, lambda i: (0,) * len(shape))
    row_spec = lambda w: pl.BlockSpec((1, NSA_G, tb, w), lambda i: (i // nt, 0, i % nt, 0))
    col_spec = pl.BlockSpec((1, NSA_G, NSA_DH, tb), lambda i: (i // nt, 0, 0, i % nt))
    row_shape = lambda w, dt: jax.ShapeDtypeStruct((B, NSA_G, T, w), dt)
    col_shape = jax.ShapeDtypeStruct((B, NSA_G, NSA_DH, T), BF16)
    return pl.pallas_call(
        functools.partial(_inproj_kernel, nt=nt),
        grid=(N // tb,),
        in_specs=[pl.BlockSpec((tb, D), lambda i: (i, 0)), full(g.shape), full(whg.shape), full(wq.shape),
                  full(wkv.shape), full(wgate.shape), full(qgc.shape), full(kg.shape)],
        out_specs=[pl.BlockSpec((tb, 4 * HG_WIDTH), lambda i: (i, 0)),
                   pl.BlockSpec((1, NSA_G, nqb, 2 * NSA_DH, NSA_R * TQ), lambda i: (i // nt, 0, i % nt, 0, 0)),
                   row_spec(NSA_DH), row_spec(NSA_DH), row_spec(2 * NSA_DH), col_spec, row_spec(2 * NSA_DH), col_spec,
                   pl.BlockSpec((256, tb), lambda i: (0, i))],
        out_shape=[jax.ShapeDtypeStruct((N, 4 * HG_WIDTH), F32),
                   jax.ShapeDtypeStruct((B, NSA_G, T // TQ, 2 * NSA_DH, NSA_R * TQ), BF16),
                   row_shape(NSA_DH, F32), row_shape(NSA_DH, F32), row_shape(2 * NSA_DH, BF16), col_shape,
                   row_shape(2 * NSA_DH, BF16), col_shape,
                   jax.ShapeDtypeStruct((256, N), F32)],
        compiler_params=_params("parallel"),
        name="inproj",
    )(x2, g, whg, wq, wkv, wgate, qgc, kg)


def _hgrn_kernel(hg_ref, lbl_ref, gain_ref, o_ref, st_ref, *, nb, cpb):
    @pl.when(pl.program_id(0) == 0)
    def _():
        st_ref[...] = jnp.zeros_like(st_ref)

    logits = lbl_ref[...]
    ex = jnp.exp(logits - jnp.max(logits, axis=0, keepdims=True))
    lb_all = ex[0:1, :] / jnp.sum(ex, axis=0, keepdims=True)
    C = HG_CHUNK
    row = lax.broadcasted_iota(jnp.int32, (C, C), 0)
    col = lax.broadcasted_iota(jnp.int32, (C, C), 1)
    tril = row >= col
    trilf = jnp.where(tril, 1.0, 0.0).astype(F32)

    def chunk(c, carry):
        r0 = pl.multiple_of(c * C, C)
        rows = pl.ds(r0, C)
        for b in range(nb):
            for h in range(HG_HEADS):
                sl = slice(HG_D * h, HG_D * (h + 1))
                q = hg_ref[b, rows, HG_D * h:HG_D * (h + 1)]
                f = hg_ref[b, rows, HG_WIDTH + HG_D * h:HG_WIDTH + HG_D * (h + 1)]
                iv = hg_ref[b, rows, 2 * HG_WIDTH + HG_D * h:2 * HG_WIDTH + HG_D * (h + 1)]
                g = hg_ref[b, rows, 3 * HG_WIDTH + HG_D * h:3 * HG_WIDTH + HG_D * (h + 1)]
                lb = lb_all[:, sl]
                forget = lb + (1.0 - lb) * jax.nn.sigmoid(f)
                logf = jnp.log(forget)
                key = 1.0 - forget
                query = jax.nn.silu(q) * HG_D ** -0.5
                G = _dot(trilf, logf, HIGHEST)
                g_ref = G[C // 2 - 1:C // 2, :]
                g_last = G[C - 1:C, :]
                qa = (query * jnp.exp(G - g_ref)).astype(BF16)
                ka = (key * jnp.exp(g_ref - G)).astype(BF16)
                a = jnp.where(tril, _dot_nt(qa, ka), 0.0)
                vb = iv.astype(BF16)
                o = _dot(a.astype(BF16), vb)
                st = st_ref[b * HG_HEADS + h]
                o = o + _dot_nt((query * jnp.exp(G)).astype(BF16), st.astype(BF16))
                kb = (key * jnp.exp(g_last - G)).astype(BF16)
                st_ref[b * HG_HEADS + h] = st * jnp.exp(g_last) + _dot_tn(vb, kb)
                on = _rms(o) * gain_ref[:, sl] * jax.nn.silu(g)
                o_ref[b, rows, sl] = on.astype(o_ref.dtype)
        return carry

    lax.fori_loop(0, cpb, chunk, 0)


def _hgrn(hg3, lb_logits, gain, cpb):
    B, T, W = hg3.shape
    rows = cpb * HG_CHUNK
    return pl.pallas_call(
        functools.partial(_hgrn_kernel, nb=B, cpb=cpb),
        grid=(T // rows,),
        in_specs=[pl.BlockSpec((B, rows, W), lambda i: (0, i, 0)),
                  pl.BlockSpec(lb_logits.shape, lambda i: (0, 0)),
                  pl.BlockSpec(gain.shape, lambda i: (0, 0))],
        out_specs=pl.BlockSpec((B, rows, HG_WIDTH), lambda i: (0, i, 0)),
        out_shape=jax.ShapeDtypeStruct((B, T, HG_WIDTH), BF16),
        scratch_shapes=[pltpu.VMEM((B * HG_HEADS, HG_D, HG_D), F32)],
        compiler_params=_params("arbitrary"),
        name="hgrn2",
    )(hg3, lb_logits, gain)


def _cmp_kernel(xk_ref, xv_ref, w1_ref, w2_ref, pe_ref, kg_ref, kc_ref, vct_ref):
    half = CMP_STRIDE * NSA_DH
    ys = []
    for idx, x_ref in enumerate((xk_ref, xv_ref)):
        x = x_ref[0]
        n = x.shape[0]
        w1 = w1_ref[idx]
        first = _dot(x, w1[:half], HIGHEST)
        second = _dot(x, w1[half:], HIGHEST)
        const = _dot(pe_ref[idx], w1, HIGHEST)[0:1, :]
        y = jax.nn.gelu(first + pltpu.roll(second, n - 1, 0) + const)
        ys.append(_dot(y, w2_ref[idx], HIGHEST))
    kc_ref[0] = (_rms(ys[0]) * kg_ref[0:1, :]).astype(BF16)
    vct_ref[0] = jnp.concatenate(ys, axis=1).T[NSA_DH:].astype(BF16)


def _compress(xk, xv, w1, w2, pe8, kg):
    BG, n, K = xk.shape
    full = lambda a: pl.BlockSpec(a.shape, lambda i: (0,) * a.ndim)
    blk = pl.BlockSpec((1, n, K), lambda i: (i, 0, 0))
    return pl.pallas_call(
        _cmp_kernel,
        grid=(BG,),
        in_specs=[blk, blk, full(w1), full(w2), full(pe8), full(kg)],
        out_specs=[pl.BlockSpec((1, n, NSA_DH), lambda i: (i, 0, 0)), pl.BlockSpec((1, NSA_DH, n), lambda i: (i, 0, 0))],
        out_shape=[jax.ShapeDtypeStruct((BG, n, NSA_DH), BF16), jax.ShapeDtypeStruct((BG, NSA_DH, n), BF16)],
        compiler_params=_params("parallel"),
        name="nsa_compress",
    )(xk, xv, w1, w2, pe8, kg)


def _nsa_kernel(qt_ref, gate_ref, kc_ref, vct_ref, ks_ref, vst_ref, kw_ref, vwt_ref, mconvt_ref, og_ref,
                o_ref, m_s, l_s, acc_s, sel_s):
    g = pl.program_id(1)
    i = pl.program_id(2)
    t0 = i * TQ
    width = NSA_R * TQ
    qa = qt_ref[0, 0, 0]
    q = qa[0:NSA_DH]
    slope_row = _alibi_slope_row(g, width)
    hcols = [slice(h * TQ, (h + 1) * TQ) for h in range(NSA_R)]

    ncp = kc_ref.shape[2]
    nn = lax.broadcasted_iota(jnp.int32, (ncp, TQ), 0)
    tt = lax.broadcasted_iota(jnp.int32, (ncp, TQ), 1)
    dist_c = t0 + tt - (nn * CMP_STRIDE + CMP_LEN - 1)
    valid_c = dist_c >= 0
    dist_cf = dist_c.astype(F32)
    s_c = _dot(kc_ref[0, 0], q)
    psum = jnp.zeros((ncp, TQ), F32)
    probs = []
    for h in range(NSA_R):
        s = jnp.where(valid_c, s_c[:, hcols[h]] - slope_row[:, hcols[h]] * dist_cf, NEG)
        m = jnp.max(s, axis=0, keepdims=True)
        m = jnp.where(m > 0.1 * NEG, m, 0.0)
        e = jnp.where(valid_c, jnp.exp(s - m), 0.0)
        p = e / jnp.maximum(jnp.sum(e, axis=0, keepdims=True), 1e-30)
        psum = psum + p
        probs.append(p.astype(BF16))
    o_cmp = _dot(vct_ref[0, 0], jnp.concatenate(probs, axis=1))

    imp = _dot(mconvt_ref[...], psum, HIGHEST)
    blk = lax.broadcasted_iota(jnp.int32, (128, TQ), 0).astype(F32)
    tcol = lax.broadcasted_iota(jnp.int32, (128, TQ), 1)
    cur = jnp.right_shift(t0 + tcol, SLC_LEN.bit_length() - 1).astype(F32)
    forced = (blk == 0.0) | (blk == cur) | (blk == cur - 1.0)
    v = jnp.where(forced, -NEG, jnp.where(blk <= cur, imp, NEG))
    sel = jnp.zeros((128, TQ), F32)
    for _ in range(SLC_TOPK):
        m = jnp.max(v, axis=0, keepdims=True)
        idx = jnp.min(jnp.where(v == m, blk, 128.0), axis=0, keepdims=True)
        pick = blk == idx
        sel = jnp.where(pick, 1.0, sel)
        v = jnp.where(pick, REMOVED, v)
    sel_s[...] = sel

    m_s[...] = jnp.full(m_s.shape, NEG, F32)
    l_s[...] = jnp.zeros(l_s.shape, F32)
    acc_s[...] = jnp.zeros(acc_s.shape, F32)
    keyr = lax.broadcasted_iota(jnp.int32, (SLC_CHUNK, TQ), 0)
    tokc = lax.broadcasted_iota(jnp.int32, (SLC_CHUNK, TQ), 1)
    blocks_per_chunk = SLC_CHUNK // SLC_LEN

    def slc_body(c, carry):
        k0 = pl.multiple_of(c * SLC_CHUNK, SLC_CHUNK)
        selx = jnp.concatenate(
            [jnp.broadcast_to(sel_s[pl.ds(blocks_per_chunk * c + j, 1), :], (SLC_LEN, TQ)) for j in range(blocks_per_chunk)],
            axis=0)
        negm = jnp.where(selx > 0.5, jnp.where(k0 + keyr <= t0 + tokc, 0.0, NEG), NEG)
        s = _dot(ks_ref[0, 0, pl.ds(k0, SLC_CHUNK), :], qa) + jnp.concatenate([negm] * NSA_R, axis=1)
        m_old = m_s[...]
        m_new = jnp.maximum(m_old, jnp.max(s, axis=0, keepdims=True))
        alpha = jnp.exp(m_old - m_new)
        p = jnp.exp(s - m_new)
        l_s[...] = alpha * l_s[...] + jnp.sum(p, axis=0, keepdims=True)
        acc_s[...] = alpha * acc_s[...] + _dot(vst_ref[0, 0, :, pl.ds(k0, SLC_CHUNK)], p.astype(BF16))
        m_s[...] = m_new
        return carry

    n_chunks = lax.shift_right_logical(t0 + TQ + SLC_CHUNK - 1, SLC_CHUNK.bit_length() - 1)
    lax.fori_loop(0, n_chunks, slc_body, 0)
    o_s = acc_s[...] / l_s[...]

    wlen = WIN + TQ
    k0 = pl.multiple_of(jnp.maximum(t0 - WIN, 0), TQ)
    wkey = lax.broadcasted_iota(jnp.int32, (wlen, TQ), 0)
    dist = t0 + lax.broadcasted_iota(jnp.int32, (wlen, TQ), 1) - (k0 + wkey)
    negm = jnp.where(dist >= 0, jnp.where(dist < WIN, 0.0, NEG), NEG)
    s = _dot(kw_ref[0, 0, pl.ds(k0, wlen), :], qa) + jnp.concatenate([negm] * NSA_R, axis=1)
    p = jnp.exp(s - jnp.max(s, axis=0, keepdims=True))
    o_w = _dot(vwt_ref[0, 0, :, pl.ds(k0, wlen)], p.astype(BF16)) / jnp.sum(p, axis=0, keepdims=True)

    gts = gate_ref[...]
    outs = []
    for h in range(NSA_R):
        cols = hcols[h]
        o = (gts[3 * h:3 * h + 1, :] * o_cmp[:, cols] + gts[3 * h + 1:3 * h + 2, :] * o_s[:, cols]
             + gts[3 * h + 2:3 * h + 3, :] * o_w[:, cols])
        outs.append(o * lax.rsqrt(jnp.mean(o * o, axis=0, keepdims=True) + EPS))
    o_ref[...] = (jnp.concatenate(outs, axis=0).T * og_ref[...]).astype(o_ref.dtype)


def _nsa(qt, gates_t, kc, vct, ks, vst, kw, vwt, mconvt, og):
    B, _, nq, _, width = qt.shape
    T = nq * TQ
    ncp = kc.shape[2]
    rows = pl.BlockSpec((1, 1, T, 2 * NSA_DH), lambda b, g, i: (b, g, 0, 0))
    cols = pl.BlockSpec((1, 1, NSA_DH, T), lambda b, g, i: (b, g, 0, 0))
    ow = NSA_R * NSA_DH
    return pl.pallas_call(
        _nsa_kernel,
        grid=(B, NSA_G, nq),
        in_specs=[pl.BlockSpec((1, 1, 1, 2 * NSA_DH, width), lambda b, g, i: (b, g, i, 0, 0)),
                  pl.BlockSpec((128, TQ), lambda b, g, i: (g, b * nq + i)),
                  pl.BlockSpec((1, 1, ncp, NSA_DH), lambda b, g, i: (b, g, 0, 0)),
                  pl.BlockSpec((1, 1, NSA_DH, ncp), lambda b, g, i: (b, g, 0, 0)),
                  rows, cols, rows, cols,
                  pl.BlockSpec(mconvt.shape, lambda b, g, i: (0, 0)),
                  pl.BlockSpec((1, ow), lambda b, g, i: (0, g))],
        out_specs=pl.BlockSpec((TQ, ow), lambda b, g, i: (b * nq + i, g)),
        out_shape=jax.ShapeDtypeStruct((B * T, NSA_G * ow), BF16),
        scratch_shapes=[pltpu.VMEM((1, width), F32), pltpu.VMEM((1, width), F32),
                        pltpu.VMEM((NSA_DH, width), F32), pltpu.VMEM((128, TQ), F32)],
        compiler_params=_params("parallel", "parallel", "arbitrary"),
        name="nsa_attn",
    )(qt, gates_t, kc, vct, ks, vst, kw, vwt, mconvt, og)


def _peer_sel_kernel(x_ref, ohg_ref, onsa_ref, wo1_ref, wo2_ref, fg_ref, wqt_ref, keys_ref,
                     h1_ref, xnt_ref, rank2_ref, e2_ref, lim1_ref, e1_ref, qt_s):
    h1 = x_ref[...] + _dot(ohg_ref[...], wo1_ref[...]) + _dot(onsa_ref[...], wo2_ref[...])
    h1_ref[...] = h1
    xnt = (_rms(h1) * fg_ref[...]).T.astype(BF16)
    xnt_ref[...] = xnt
    qt_s[...] = _dot(wqt_ref[...], xnt)
    tb = xnt.shape[1]
    K = PEER_TOPK
    rowi = lax.broadcasted_iota(jnp.int32, (PEER_NKEYS, tb), 0).astype(F32)
    k1i = lax.broadcasted_iota(jnp.int32, (K, tb), 0).astype(F32)
    r = lax.broadcasted_iota(jnp.int32, (K + 7 * 8 + 8, tb), 0)
    ci = jnp.where(r < K, r, jnp.where(r < K + 56, K * (1 + jnp.right_shift(r - K, 3)) + jnp.bitwise_and(r - K, 7),
                                       K * (r - (K + 56) + 8))).astype(F32)

    def topk(s):
        rank = jnp.full(s.shape, float(PEER_NKEYS), F32)
        tops = []
        for k in range(K):
            m = jnp.max(s, axis=0, keepdims=True)
            idx = jnp.min(jnp.where(s == m, rowi, float(PEER_NKEYS)), axis=0, keepdims=True)
            pick = rowi == idx
            rank = jnp.where(pick, float(k), rank)
            s = jnp.where(pick, REMOVED, s)
            tops.append(m)
        return rank, jnp.concatenate(tops, axis=0)

    def head(h, carry):
        r = pl.multiple_of(h * 2 * PEER_NKEYS, 2 * PEER_NKEYS)
        s1 = _dot(keys_ref[2 * h], qt_s[pl.ds(r, PEER_NKEYS), :].astype(BF16))
        s2 = _dot(keys_ref[2 * h + 1], qt_s[pl.ds(r + PEER_NKEYS, PEER_NKEYS), :].astype(BF16))
        r1, t1 = topk(s1)
        r2, t2 = topk(s2)
        cand = jnp.concatenate([t1[0:1, :] + t2] + [t1[k:k + 1, :] + t2[0:8, :] for k in range(1, 8)]
                               + [t1[8:K, :] + t2[0:1, :]], axis=0)
        m0 = t1[0:1, :] + t2[0:1, :]
        lim = jnp.zeros((K, tb), F32)
        z = jnp.zeros((1, tb), F32)
        for k in range(K):
            m = jnp.max(cand, axis=0, keepdims=True)
            idx = jnp.min(jnp.where(cand == m, ci, float(K * K)), axis=0, keepdims=True)
            cand = jnp.where(ci == idx, REMOVED, cand)
            lim = lim + jnp.where(k1i == jnp.floor(idx * (1.0 / K)), 1.0, 0.0)
            z = z + jnp.exp(m - m0)
        lim1 = jnp.zeros((PEER_NKEYS, tb), F32)
        for k in range(K):
            lim1 = lim1 + jnp.where(r1 == float(k), lim[k:k + 1, :], 0.0)
        e1 = jnp.exp(s1 - t1[0:1, :])
        e2 = jnp.exp(s2 - t2[0:1, :]) / z
        for ts in range(tb // 128):
            cols = slice(128 * ts, 128 * (ts + 1))
            rank2_ref[h, ts] = pltpu.bitcast(r2[:, cols].astype(BF16), jnp.uint32)
            lim1_ref[h, ts] = lim1[:, cols]
            e1_ref[h, ts] = e1[:, cols]
            e2_ref[h, ts] = pltpu.bitcast(e2[:, cols].astype(BF16), jnp.uint32)
        return carry

    lax.fori_loop(0, PEER_HEADS, head, 0)


def _peer_select(x2, ohg, onsa, wo1, wo2, fg, wqt, keys, tb):
    N, D = x2.shape
    full = lambda a: pl.BlockSpec(a.shape, lambda i: (0,) * a.ndim)
    sel_spec = pl.BlockSpec((PEER_HEADS, tb // 128, PEER_NKEYS, 128), lambda i: (0, i, 0, 0))
    sel_shape = jax.ShapeDtypeStruct((PEER_HEADS, N // 128, PEER_NKEYS, 128), F32)
    packed_spec = pl.BlockSpec((PEER_HEADS, tb // 128, PEER_NKEYS // 2, 128), lambda i: (0, i, 0, 0))
    packed_shape = jax.ShapeDtypeStruct((PEER_HEADS, N // 128, PEER_NKEYS // 2, 128), jnp.uint32)
    return pl.pallas_call(
        _peer_sel_kernel,
        grid=(N // tb,),
        in_specs=[pl.BlockSpec((tb, D), lambda i: (i, 0)), pl.BlockSpec((tb, ohg.shape[1]), lambda i: (i, 0)),
                  pl.BlockSpec((tb, onsa.shape[1]), lambda i: (i, 0)),
                  full(wo1), full(wo2), full(fg), full(wqt), full(keys)],
        out_specs=[pl.BlockSpec((tb, D), lambda i: (i, 0)), pl.BlockSpec((D, tb), lambda i: (0, i)),
                   packed_spec, packed_spec, sel_spec, sel_spec],
        out_shape=[jax.ShapeDtypeStruct((N, D), F32), jax.ShapeDtypeStruct((D, N), BF16),
                   packed_shape, packed_shape, sel_shape, sel_shape],
        scratch_shapes=[pltpu.VMEM((wqt.shape[0], tb), F32)],
        compiler_params=_params("parallel"),
        name="peer_select",
    )(x2, ohg, onsa, wo1, wo2, fg, wqt, keys)


def _peer_dense_kernel(xnt_ref, u_ref, vt_ref, rank2_ref, e2_ref, lim1_ref, e1_ref, h1_ref, p_ref,
                       pproj_ref, pgn_ref, pgw_ref, out_ref, acc_ref, at_s, wg_s, *, eb):
    e = pl.program_id(1)

    @pl.when(e == 0)
    def _():
        acc_ref[...] = jnp.zeros_like(acc_ref)

    at_s[...] = _dot(u_ref[...], xnt_ref[...])
    groups = eb // PEER_NKEYS
    half = PEER_NKEYS // 2
    for ts in range(at_s.shape[1] // 128):
        cols = slice(128 * ts, 128 * (ts + 1))
        for j in range(groups):
            i1 = e * groups + j
            rows = slice(PEER_NKEYS * j, PEER_NKEYS * (j + 1))
            wt = None
            for h in range(PEER_HEADS):
                r2 = pltpu.bitcast(rank2_ref[h, half * ts:half * (ts + 1), :], BF16)
                w2 = pltpu.bitcast(e2_ref[h, half * ts:half * (ts + 1), :], BF16)
                lim = jnp.broadcast_to(lim1_ref[h, ts, pl.ds(i1, 1), :].astype(BF16), (PEER_NKEYS, 128))
                w1 = jnp.broadcast_to(e1_ref[h, ts, pl.ds(i1, 1), :].astype(BF16), (PEER_NKEYS, 128))
                w = jnp.where(r2 < lim, w2, 0.0) * w1
                wt = w if wt is None else wt + w
            wg_s[rows, cols] = wt * jax.nn.gelu(at_s[rows, cols]).astype(BF16)
    acc_ref[...] += _dot(vt_ref[...], wg_s[...])

    @pl.when(e == pl.num_programs(1) - 1)
    def _():
        h2 = h1_ref[...] + acc_ref[...].T
        gate = jax.nn.sigmoid(_dot((_rms(h2) * pgn_ref[...]).astype(BF16), pgw_ref[...]))
        out_ref[...] = h2 + _dot(p_ref[...].astype(BF16), pproj_ref[...]) * gate


def _peer_dense(xnt, u, vt, rank2, e2, lim1, e1, h1, p2, pproj, pgn, pgw, tb, eb):
    D, N = xnt.shape
    ne = u.shape[0] // eb
    full = lambda a: pl.BlockSpec(a.shape, lambda t, e: (0,) * a.ndim)
    sel_spec = pl.BlockSpec((PEER_HEADS, tb // 128, PEER_NKEYS, 128), lambda t, e: (0, t, 0, 0))
    packed_spec = pl.BlockSpec((PEER_HEADS, tb // 128 * (PEER_NKEYS // 2), 128), lambda t, e: (0, t, 0))
    rank2 = rank2.reshape(PEER_HEADS, -1, 128)
    e2 = e2.reshape(PEER_HEADS, -1, 128)
    return pl.pallas_call(
        functools.partial(_peer_dense_kernel, eb=eb),
        grid=(N // tb, ne),
        in_specs=[pl.BlockSpec((D, tb), lambda t, e: (0, t)), pl.BlockSpec((eb, D), lambda t, e: (e, 0)),
                  pl.BlockSpec((D, eb), lambda t, e: (0, e)), packed_spec, packed_spec, sel_spec, sel_spec,
                  pl.BlockSpec((tb, D), lambda t, e: (t, 0)), pl.BlockSpec((tb, p2.shape[1]), lambda t, e: (t, 0)),
                  full(pproj), full(pgn), full(pgw)],
        out_specs=pl.BlockSpec((tb, D), lambda t, e: (t, 0)),
        out_shape=jax.ShapeDtypeStruct((N, D), F32),
        scratch_shapes=[pltpu.VMEM((D, tb), F32), pltpu.VMEM((eb, tb), F32), pltpu.VMEM((eb, tb), BF16)],
        compiler_params=_params("parallel", "arbitrary"),
        name="peer_dense",
    )(xnt, u, vt, rank2, e2, lim1, e1, h1, p2, pproj, pgn, pgw)


def _conv_matrix(ncp):
    ratio = SLC_LEN // CMP_STRIDE
    weights = np.convolve(np.ones(ratio), np.ones(CMP_LEN // CMP_STRIDE))
    m = np.zeros((ncp, 128), np.float32)
    for j in range(128):
        for o, w in enumerate(weights):
            if ratio * j + o < ncp:
                m[ratio * j + o, j] = w
    return jnp.asarray(m)


def _block(n, pref):
    while n % pref:
        pref //= 2
    return pref


def kernel(x, p, mix_norm, w_in, hg_lb_logits, hg_out_norm, nsa_q_norm, nsa_k_norm, cmp_pe, cmp_w1, cmp_w2,
           nsa_out_norm, w_out, ffn_norm, peer_wq, peer_keys, peer_u, peer_v, ple_proj, ple_gate_norm, ple_gate_w):
    B, T, D = x.shape
    N = B * T
    x2 = x.reshape(N, D)
    nsa_w = NSA_HEADS * NSA_DH
    kvw = NSA_G * NSA_DH

    w = w_in[0]
    c0 = 4 * HG_WIDTH
    whg = w[:, :c0].astype(BF16)
    wq = w[:, c0:c0 + nsa_w].astype(BF16)
    wkv = w[:, c0 + nsa_w:c0 + nsa_w + 6 * kvw].astype(BF16)
    wg = w[:, c0 + nsa_w + 6 * kvw:]
    per_g = 3 * NSA_R
    wgate = jnp.zeros((D, 256), F32)
    for g in range(NSA_G):
        wgate = wgate.at[:, 128 * g:128 * g + per_g].set(wg[:, per_g * g:per_g * (g + 1)])
    wgate = wgate.astype(BF16)

    hg, qt, kcr, vcr, ks, vst, kw, vwt, gates_t = _inproj(
        x2, mix_norm, whg, wq, wkv, wgate, nsa_q_norm.reshape(NSA_DH, 1), nsa_k_norm[0], B, T, _block(T, 256))

    o_hg = _hgrn(hg.reshape(B, T, 4 * HG_WIDTH), hg_lb_logits, hg_out_norm, _block(T // HG_CHUNK, 4))

    ncp = T // CMP_STRIDE
    stride_rows = lambda a: a.reshape(B * NSA_G, ncp, CMP_STRIDE * NSA_DH)
    pe8 = jnp.broadcast_to(cmp_pe[0].reshape(2, 1, CMP_LEN * NSA_DH), (2, 8, CMP_LEN * NSA_DH))
    kc, vct = _compress(stride_rows(kcr), stride_rows(vcr), cmp_w1[0], cmp_w2[0], pe8, nsa_k_norm[0])
    kc = kc.reshape(B, NSA_G, ncp, NSA_DH)
    vct = vct.reshape(B, NSA_G, NSA_DH, ncp)

    o_nsa = _nsa(qt, gates_t, kc, vct, ks, vst, kw, vwt, _conv_matrix(ncp).T, nsa_out_norm)

    wo = w_out[0].astype(BF16)
    wqt = peer_wq[0].T.astype(BF16)
    keys = peer_keys[0].reshape(2 * PEER_HEADS, PEER_NKEYS, -1).astype(BF16)
    h1, xnt, rank2, e2, lim1, e1 = _peer_select(
        x2, o_hg.reshape(N, HG_WIDTH), o_nsa, wo[:HG_WIDTH], wo[HG_WIDTH:], ffn_norm, wqt, keys, _block(N, 256))

    out = _peer_dense(xnt, peer_u[0].astype(BF16), peer_v[0].T.astype(BF16), rank2, e2, lim1, e1, h1,
                      p[0].reshape(N, -1), ple_proj[0].astype(BF16), ple_gate_norm, ple_gate_w[0].astype(BF16),
                      _block(N, 512), 512)
    return out.reshape(B, T, D)
```

```python
import functools

import numpy as np
import jax
import jax.numpy as jnp
from jax import lax
from jax.experimental import pallas as pl
from jax.experimental.pallas import tpu as pltpu

F32 = jnp.float32
BF16 = jnp.bfloat16
HIGHEST = lax.Precision.HIGHEST
EPS = 1e-6
NEG = -1e30
REMOVED = -3e38
MARK_STEP = 2e36
MARK_LIMIT = -2.5e38

HG_HEADS = 4
HG_D = 128
HG_CHUNK = 64
HG_WIDTH = HG_HEADS * HG_D
NSA_DH = 64
NSA_HEADS = 8
NSA_G = 2
NSA_R = NSA_HEADS // NSA_G
CMP_LEN = 32
CMP_STRIDE = 16
SLC_LEN = 64
SLC_TOPK = 16
WIN = 512
TQ = 128
SLC_CHUNK = 256
PEER_HEADS = 8
PEER_NKEYS = 128
PEER_TOPK = 16
VMEM_LIMIT = 56 * 1024 * 1024


def _dot(a, b, precision=None):
    return lax.dot_general(a, b, (((1,), (0,)), ((), ())), preferred_element_type=F32, precision=precision)


def _dot_nt(a, b):
    return lax.dot_general(a, b, (((1,), (1,)), ((), ())), preferred_element_type=F32)


def _dot_tn(a, b):
    return lax.dot_general(a, b, (((0,), (0,)), ((), ())), preferred_element_type=F32)


def _rms(x):
    return x * lax.rsqrt(jnp.mean(x * x, axis=-1, keepdims=True) + EPS)


def _params(*sem):
    return pltpu.CompilerParams(dimension_semantics=sem, vmem_limit_bytes=VMEM_LIMIT)


def _alibi_slope_row(g, width):
    lane_h = jnp.right_shift(lax.broadcasted_iota(jnp.int32, (1, width), 1), TQ.bit_length() - 1)
    row = jnp.zeros((1, width), F32)
    for h in range(NSA_R):
        row = jnp.where(lane_h == h, jnp.where(g == 0, 2.0 ** -(h + 1), 2.0 ** -(NSA_R + h + 1)).astype(F32), row)
    return row


def _inproj_kernel(x_ref, g_ref, whg_ref, wq_ref, wkv_ref, wgate_ref, qgc_ref, kg_ref,
                   hg_ref, qt_ref, kcr_ref, vcr_ref, ks_ref, vst_ref, kw_ref, vwt_ref, gate_ref, *, nt):
    a = (_rms(x_ref[...]) * g_ref[...]).astype(BF16)
    hg_ref[...] = _dot(a, whg_ref[...])
    tb = a.shape[0]
    nqb = tb // TQ
    width = NSA_R * TQ

    qt = _dot(a, wq_ref[...]).T
    aug_row = lax.broadcasted_iota(jnp.int32, (NSA_DH, width), 0)
    for g in range(NSA_G):
        slope = _alibi_slope_row(g, width)
        aug = jnp.where(aug_row == 0, slope, jnp.where(aug_row == 1, slope * TQ, 0.0)).astype(BF16)
        for j in range(nqb):
            qt_ref[0, g, j, NSA_DH:2 * NSA_DH, :] = aug
        for h in range(NSA_R):
            blk = qt[NSA_DH * (NSA_R * g + h):NSA_DH * (NSA_R * g + h + 1)]
            r = lax.rsqrt(jnp.mean(blk * blk, axis=0, keepdims=True) + EPS)
            qn = (blk * r * qgc_ref[...] * NSA_DH ** -0.5).astype(BF16)
            for j in range(nqb):
                qt_ref[0, g, j, 0:NSA_DH, TQ * h:TQ * (h + 1)] = qn[:, TQ * j:TQ * (j + 1)]

    kv = _dot(a, wkv_ref[...])
    kvt = kv.T
    pos = (pl.program_id(0) % nt) * tb + lax.broadcasted_iota(jnp.int32, (tb, NSA_DH), 0)
    aug_col = lax.broadcasted_iota(jnp.int32, (tb, NSA_DH), 1)
    pos_aug = jnp.where(aug_col == 0, jnp.bitwise_and(pos, TQ - 1),
                        jnp.where(aug_col == 1, jnp.right_shift(pos, TQ.bit_length() - 1), 0)).astype(F32)

    def part(j, g):
        return kv[:, 128 * j + NSA_DH * g:128 * j + NSA_DH * (g + 1)]

    def part_t(j, g):
        return kvt[128 * j + NSA_DH * g:128 * j + NSA_DH * (g + 1)]

    for g in range(NSA_G):
        kcr_ref[0, g] = part(0, g)
        vcr_ref[0, g] = part(1, g)
        ks_ref[0, g] = jnp.concatenate([_rms(part(2, g)) * kg_ref[1:2, :], pos_aug], axis=1).astype(BF16)
        vst_ref[0, g] = part_t(3, g).astype(BF16)
        kw_ref[0, g] = jnp.concatenate([_rms(part(4, g)) * kg_ref[2:3, :], pos_aug], axis=1).astype(BF16)
        vwt_ref[0, g] = part_t(5, g).astype(BF16)
    gate_ref[...] = jax.nn.sigmoid(_dot(a, wgate_ref[...])).T


def _inproj(x2, g, whg, wq, wkv, wgate, qgc, kg, B, T, tb):
    N, D = x2.shape
    nt = T // tb
    nqb = tb // TQ
    full = lambda shape: pl.BlockSpec(shape, lambda i: (0,) * len(shape))
    row_spec = lambda w: pl.BlockSpec((1, NSA_G, tb, w), lambda i: (i // nt, 0, i % nt, 0))
    col_spec = pl.BlockSpec((1, NSA_G, NSA_DH, tb), lambda i: (i // nt, 0, 0, i % nt))
    row_shape = lambda w, dt: jax.ShapeDtypeStruct((B, NSA_G, T, w), dt)
    col_shape = jax.ShapeDtypeStruct((B, NSA_G, NSA_DH, T), BF16)
    return pl.pallas_call(
        functools.partial(_inproj_kernel, nt=nt),
        grid=(N // tb,),
        in_specs=[pl.BlockSpec((tb, D), lambda i: (i, 0)), full(g.shape), full(whg.shape), full(wq.shape),
                  full(wkv.shape), full(wgate.shape), full(qgc.shape), full(kg.shape)],
        out_specs=[pl.BlockSpec((tb, 4 * HG_WIDTH), lambda i: (i, 0)),
                   pl.BlockSpec((1, NSA_G, nqb, 2 * NSA_DH, NSA_R * TQ), lambda i: (i // nt, 0, i % nt, 0, 0)),
                   row_spec(NSA_DH), row_spec(NSA_DH), row_spec(2 * NSA_DH), col_spec, row_spec(2 * NSA_DH), col_spec,
                   pl.BlockSpec((256, tb), lambda i: (0, i))],
        out_shape=[jax.ShapeDtypeStruct((N, 4 * HG_WIDTH), F32),
                   jax.ShapeDtypeStruct((B, NSA_G, T // TQ, 2 * NSA_DH, NSA_R * TQ), BF16),
                   row_shape(NSA_DH, F32), row_shape(NSA_DH, F32), row_shape(2 * NSA_DH, BF16), col_shape,
                   row_shape(2 * NSA_DH, BF16), col_shape,
                   jax.ShapeDtypeStruct((256, N), F32)],
        compiler_params=_params("parallel"),
        name="inproj",
    )(x2, g, whg, wq, wkv, wgate, qgc, kg)


def _hgrn_kernel(hg_ref, lbl_ref, gain_ref, o_ref, st_ref, *, nb, cpb):
    @pl.when(pl.program_id(0) == 0)
    def _():
        st_ref[...] = jnp.zeros_like(st_ref)

    logits = lbl_ref[...]
    ex = jnp.exp(logits - jnp.max(logits, axis=0, keepdims=True))
    lb_all = ex[0:1, :] / jnp.sum(ex, axis=0, keepdims=True)
    C = HG_CHUNK
    row = lax.broadcasted_iota(jnp.int32, (C, C), 0)
    col = lax.broadcasted_iota(jnp.int32, (C, C), 1)
    tril = row >= col
    trilf = jnp.where(tril, 1.0, 0.0).astype(F32)

    def chunk(c, carry):
        r0 = pl.multiple_of(c * C, C)
        rows = pl.ds(r0, C)
        for b in range(nb):
            for h in range(HG_HEADS):
                sl = slice(HG_D * h, HG_D * (h + 1))
                q, f, iv, g = [hg_ref[b, rows, j * HG_WIDTH + HG_D * h:j * HG_WIDTH + HG_D * (h + 1)] for j in range(4)]
                st = st_ref[b * HG_HEADS + h]
                lb = lb_all[:, sl]
                forget = lb + (1.0 - lb) * jax.nn.sigmoid(f)
                logf = jnp.log(forget)
                key = 1.0 - forget
                query = jax.nn.silu(q) * HG_D ** -0.5
                G = _dot(trilf, logf, HIGHEST)
                g_ref = G[C // 2 - 1:C // 2, :]
                g_last = G[C - 1:C, :]
                qa = (query * jnp.exp(G - g_ref)).astype(BF16)
                ka = (key * jnp.exp(g_ref - G)).astype(BF16)
                a = jnp.where(tril, _dot_nt(qa, ka), 0.0)
                vb = iv.astype(BF16)
                o = _dot(a.astype(BF16), vb)
                o = o + _dot_nt((query * jnp.exp(G)).astype(BF16), st.astype(BF16))
                kb = (key * jnp.exp(g_last - G)).astype(BF16)
                st_ref[b * HG_HEADS + h] = st * jnp.exp(g_last) + _dot_tn(vb, kb)
                on = _rms(o) * gain_ref[:, sl] * jax.nn.silu(g)
                o_ref[b, rows, sl] = on.astype(o_ref.dtype)
        return carry

    lax.fori_loop(0, cpb, chunk, 0)


def _hgrn(hg3, lb_logits, gain, cpb):
    B, T, W = hg3.shape
    rows = cpb * HG_CHUNK
    return pl.pallas_call(
        functools.partial(_hgrn_kernel, nb=B, cpb=cpb),
        grid=(T // rows,),
        in_specs=[pl.BlockSpec((B, rows, W), lambda i: (0, i, 0)),
                  pl.BlockSpec(lb_logits.shape, lambda i: (0, 0)),
                  pl.BlockSpec(gain.shape, lambda i: (0, 0))],
        out_specs=pl.BlockSpec((B, rows, HG_WIDTH), lambda i: (0, i, 0)),
        out_shape=jax.ShapeDtypeStruct((B, T, HG_WIDTH), BF16),
        scratch_shapes=[pltpu.VMEM((B * HG_HEADS, HG_D, HG_D), F32)],
        compiler_params=_params("arbitrary"),
        name="hgrn2",
    )(hg3, lb_logits, gain)


def _cmp_kernel(xk_ref, xv_ref, w1_ref, w2_ref, pe_ref, kg_ref, kc_ref, vct_ref):
    half = CMP_STRIDE * NSA_DH
    ys = []
    for idx, x_ref in enumerate((xk_ref, xv_ref)):
        x = x_ref[0]
        n = x.shape[0]
        w1 = w1_ref[idx]
        first = _dot(x, w1[:half], HIGHEST)
        second = _dot(x, w1[half:], HIGHEST)
        const = _dot(pe_ref[idx], w1, HIGHEST)[0:1, :]
        y = jax.nn.gelu(first + pltpu.roll(second, n - 1, 0) + const)
        ys.append(_dot(y, w2_ref[idx], HIGHEST))
    kc_ref[0] = (_rms(ys[0]) * kg_ref[0:1, :]).astype(BF16)
    vct_ref[0] = jnp.concatenate(ys, axis=1).T[NSA_DH:].astype(BF16)


def _compress(xk, xv, w1, w2, pe8, kg):
    BG, n, K = xk.shape
    full = lambda a: pl.BlockSpec(a.shape, lambda i: (0,) * a.ndim)
    blk = pl.BlockSpec((1, n, K), lambda i: (i, 0, 0))
    return pl.pallas_call(
        _cmp_kernel,
        grid=(BG,),
        in_specs=[blk, blk, full(w1), full(w2), full(pe8), full(kg)],
        out_specs=[pl.BlockSpec((1, n, NSA_DH), lambda i: (i, 0, 0)), pl.BlockSpec((1, NSA_DH, n), lambda i: (i, 0, 0))],
        out_shape=[jax.ShapeDtypeStruct((BG, n, NSA_DH), BF16), jax.ShapeDtypeStruct((BG, NSA_DH, n), BF16)],
        compiler_params=_params("parallel"),
        name="nsa_compress",
    )(xk, xv, w1, w2, pe8, kg)


def _nsa_kernel(qt_ref, gate_ref, kc_ref, vct_ref, ks_ref, vst_ref, kw_ref, vwt_ref, mconvt_ref, og_ref,
                o_ref, m_s, l_s, acc_s, sel_s, sa_s, sb_s):
    g = pl.program_id(1)
    i = pl.program_id(2)
    t0 = i * TQ
    width = NSA_R * TQ
    qa = qt_ref[0, 0, 0]
    q = qa[0:NSA_DH]
    slope_row = _alibi_slope_row(g, width)
    hcols = [slice(h * TQ, (h + 1) * TQ) for h in range(NSA_R)]

    ncp = kc_ref.shape[2]
    nn = lax.broadcasted_iota(jnp.int32, (ncp, TQ), 0)
    tt = lax.broadcasted_iota(jnp.int32, (ncp, TQ), 1)
    dist_c = t0 + tt - (nn * CMP_STRIDE + CMP_LEN - 1)
    valid_c = dist_c >= 0
    dist_cf = dist_c.astype(F32)
    s_c = _dot(kc_ref[0, 0], q)
    psum = jnp.zeros((ncp, TQ), F32)
    probs = []
    for h in range(NSA_R):
        s = jnp.where(valid_c, s_c[:, hcols[h]] - slope_row[:, hcols[h]] * dist_cf, NEG)
        m = jnp.max(s, axis=0, keepdims=True)
        m = jnp.where(m > 0.1 * NEG, m, 0.0)
        e = jnp.where(valid_c, jnp.exp(s - m), 0.0)
        p = e / jnp.maximum(jnp.sum(e, axis=0, keepdims=True), 1e-30)
        psum = psum + p
        probs.append(p.astype(BF16))
    o_cmp = _dot(vct_ref[0, 0], jnp.concatenate(probs, axis=1))

    imp = _dot(mconvt_ref[...], psum, HIGHEST)
    blk = lax.broadcasted_iota(jnp.int32, (128, TQ), 0).astype(F32)
    tcol = lax.broadcasted_iota(jnp.int32, (128, TQ), 1)
    cur = jnp.right_shift(t0 + tcol, SLC_LEN.bit_length() - 1).astype(F32)
    forced = (blk == 0.0) | (blk == cur) | (blk == cur - 1.0)
    v = jnp.where(forced, -NEG, jnp.where(blk <= cur, imp, NEG))
    sel = jnp.zeros((128, TQ), F32)
    for _ in range(SLC_TOPK):
        m = jnp.max(v, axis=0, keepdims=True)
        idx = jnp.min(jnp.where(v == m, blk, 128.0), axis=0, keepdims=True)
        pick = blk == idx
        sel = jnp.where(pick, 1.0, sel)
        v = jnp.where(pick, REMOVED, v)
    sel_s[...] = sel

    m_s[...] = jnp.full(m_s.shape, NEG, F32)
    l_s[...] = jnp.zeros(l_s.shape, F32)
    acc_s[...] = jnp.zeros(acc_s.shape, F32)
    keyr = lax.broadcasted_iota(jnp.int32, (SLC_CHUNK, TQ), 0)
    tokc = lax.broadcasted_iota(jnp.int32, (SLC_CHUNK, TQ), 1)
    blocks_per_chunk = SLC_CHUNK // SLC_LEN

    last_k0 = ks_ref.shape[2] - SLC_CHUNK

    def scores(c):
        k0 = pl.multiple_of(jnp.minimum(c * SLC_CHUNK, last_k0), SLC_CHUNK)
        return _dot(ks_ref[0, 0, pl.ds(k0, SLC_CHUNK), :], qa)

    def update(s_ref, c):
        k0 = pl.multiple_of(c * SLC_CHUNK, SLC_CHUNK)
        selx = jnp.concatenate(
            [jnp.broadcast_to(sel_s[pl.ds(blocks_per_chunk * c + j, 1), :], (SLC_LEN, TQ)) for j in range(blocks_per_chunk)],
            axis=0)
        negm = jnp.where(selx > 0.5, jnp.where(k0 + keyr <= t0 + tokc, 0.0, NEG), NEG)
        s = s_ref[...] + jnp.concatenate([negm] * NSA_R, axis=1)
        m_old = m_s[...]
        m_new = jnp.maximum(m_old, jnp.max(s, axis=0, keepdims=True))
        alpha = jnp.exp(m_old - m_new)
        p = jnp.exp(s - m_new)
        l_s[...] = alpha * l_s[...] + jnp.sum(p, axis=0, keepdims=True)
        acc_s[...] = alpha * acc_s[...] + _dot(vst_ref[0, 0, :, pl.ds(k0, SLC_CHUNK)], p.astype(BF16))
        m_s[...] = m_new

    sa_s[...] = scores(0)

    def slc_pair(c, carry):
        sb_s[...] = scores(2 * c + 1)
        update(sa_s, 2 * c)
        sa_s[...] = scores(2 * c + 2)
        update(sb_s, 2 * c + 1)
        return carry

    n_pairs = lax.shift_right_logical(t0 + TQ + 2 * SLC_CHUNK - 1, SLC_CHUNK.bit_length())
    lax.fori_loop(0, n_pairs, slc_pair, 0)
    o_s = acc_s[...] / l_s[...]

    wlen = WIN + TQ
    k0 = pl.multiple_of(jnp.maximum(t0 - WIN, 0), TQ)
    wkey = lax.broadcasted_iota(jnp.int32, (wlen, TQ), 0)
    dist = t0 + lax.broadcasted_iota(jnp.int32, (wlen, TQ), 1) - (k0 + wkey)
    negm = jnp.where(dist >= 0, jnp.where(dist < WIN, 0.0, NEG), NEG)
    s = _dot(kw_ref[0, 0, pl.ds(k0, wlen), :], qa) + jnp.concatenate([negm] * NSA_R, axis=1)
    p = jnp.exp(s - jnp.max(s, axis=0, keepdims=True))
    o_w = _dot(vwt_ref[0, 0, :, pl.ds(k0, wlen)], p.astype(BF16)) / jnp.sum(p, axis=0, keepdims=True)

    gts = gate_ref[...]
    outs = []
    for h in range(NSA_R):
        cols = hcols[h]
        o = (gts[3 * h:3 * h + 1, :] * o_cmp[:, cols] + gts[3 * h + 1:3 * h + 2, :] * o_s[:, cols]
             + gts[3 * h + 2:3 * h + 3, :] * o_w[:, cols])
        outs.append(o * lax.rsqrt(jnp.mean(o * o, axis=0, keepdims=True) + EPS))
    o_ref[...] = (jnp.concatenate(outs, axis=0).T * og_ref[...]).astype(o_ref.dtype)


def _nsa(qt, gates_t, kc, vct, ks, vst, kw, vwt, mconvt, og):
    B, _, nq, _, width = qt.shape
    T = nq * TQ
    ncp = kc.shape[2]
    rows = pl.BlockSpec((1, 1, T, 2 * NSA_DH), lambda b, g, i: (b, g, 0, 0))
    cols = pl.BlockSpec((1, 1, NSA_DH, T), lambda b, g, i: (b, g, 0, 0))
    ow = NSA_R * NSA_DH
    return pl.pallas_call(
        _nsa_kernel,
        grid=(B, NSA_G, nq),
        in_specs=[pl.BlockSpec((1, 1, 1, 2 * NSA_DH, width), lambda b, g, i: (b, g, i, 0, 0)),
                  pl.BlockSpec((128, TQ), lambda b, g, i: (g, b * nq + i)),
                  pl.BlockSpec((1, 1, ncp, NSA_DH), lambda b, g, i: (b, g, 0, 0)),
                  pl.BlockSpec((1, 1, NSA_DH, ncp), lambda b, g, i: (b, g, 0, 0)),
                  rows, cols, rows, cols,
                  pl.BlockSpec(mconvt.shape, lambda b, g, i: (0, 0)),
                  pl.BlockSpec((1, ow), lambda b, g, i: (0, g))],
        out_specs=pl.BlockSpec((TQ, ow), lambda b, g, i: (b * nq + i, g)),
        out_shape=jax.ShapeDtypeStruct((B * T, NSA_G * ow), BF16),
        scratch_shapes=[pltpu.VMEM((1, width), F32), pltpu.VMEM((1, width), F32),
                        pltpu.VMEM((NSA_DH, width), F32), pltpu.VMEM((128, TQ), F32),
                        pltpu.VMEM((SLC_CHUNK, width), F32), pltpu.VMEM((SLC_CHUNK, width), F32)],
        compiler_params=_params("parallel", "parallel", "arbitrary"),
        name="nsa_attn",
    )(qt, gates_t, kc, vct, ks, vst, kw, vwt, mconvt, og)


def _peer_sel_kernel(x_ref, ohg_ref, onsa_ref, wo1_ref, wo2_ref, fg_ref, wqt_ref, keys_ref,
                     h1_ref, xnt_ref, rank2_ref, e2_ref, lim1_ref, e1_ref, qt_s, r1_s, r2_s, lim_s, top_s):
    h1 = x_ref[...] + _dot(ohg_ref[...], wo1_ref[...]) + _dot(onsa_ref[...], wo2_ref[...])
    h1_ref[...] = h1
    xnt = (_rms(h1) * fg_ref[...]).T.astype(BF16)
    xnt_ref[...] = xnt
    qt_s[...] = _dot(wqt_ref[...], xnt)
    tb = xnt.shape[1]
    K = PEER_TOPK
    rowi = lax.broadcasted_iota(jnp.int32, (PEER_NKEYS, tb), 0).astype(F32)
    k1i = lax.broadcasted_iota(jnp.int32, (K, tb), 0).astype(F32)
    r = lax.broadcasted_iota(jnp.int32, (K + 7 * 8 + 8, tb), 0)
    ci = jnp.where(r < K, r, jnp.where(r < K + 56, K * (1 + jnp.right_shift(r - K, 3)) + jnp.bitwise_and(r - K, 7),
                                       K * (r - (K + 56) + 8))).astype(F32)

    mark = [REMOVED + k * MARK_STEP for k in range(K)]

    def topk_exact(s):
        rank = jnp.full(s.shape, float(PEER_NKEYS), F32)
        tops = []
        for k in range(K):
            m = jnp.max(s, axis=0, keepdims=True)
            idx = jnp.min(jnp.where(s == m, rowi, float(PEER_NKEYS)), axis=0, keepdims=True)
            pick = rowi == idx
            rank = jnp.where(pick, float(k), rank)
            s = jnp.where(pick, REMOVED, s)
            tops.append(m)
        return rank, jnp.concatenate(tops, axis=0)

    def topk_fast(s):
        tops = []
        for k in range(K):
            m = jnp.max(s, axis=0, keepdims=True)
            s = jnp.where(s == m, mark[k], s)
            tops.append(m)
        taken = s < MARK_LIMIT
        rank = jnp.where(taken, jnp.floor((s - mark[0]) * (1.0 / MARK_STEP) + 0.5), float(PEER_NKEYS))
        count = jnp.sum(jnp.where(taken, 1.0, 0.0), axis=0, keepdims=True)
        return rank, jnp.concatenate(tops, axis=0), count

    def candidates(t1, t2):
        return jnp.concatenate([t1[0:1, :] + t2] + [t1[k:k + 1, :] + t2[0:8, :] for k in range(1, 8)]
                               + [t1[8:K, :] + t2[0:1, :]], axis=0)

    def pairs_exact(t1, t2):
        cand = candidates(t1, t2)
        m0 = t1[0:1, :] + t2[0:1, :]
        lim = jnp.zeros((K, tb), F32)
        z = jnp.zeros((1, tb), F32)
        for k in range(K):
            m = jnp.max(cand, axis=0, keepdims=True)
            idx = jnp.min(jnp.where(cand == m, ci, float(K * K)), axis=0, keepdims=True)
            cand = jnp.where(ci == idx, REMOVED, cand)
            lim = lim + jnp.where(k1i == jnp.floor(idx * (1.0 / K)), 1.0, 0.0)
            z = z + jnp.exp(m - m0)
        return lim, z

    def pairs_fast(t1, t2):
        cand = candidates(t1, t2)
        m0 = t1[0:1, :] + t2[0:1, :]
        z = jnp.zeros((1, tb), F32)
        for k in range(K):
            m = jnp.max(cand, axis=0, keepdims=True)
            cand = jnp.where(cand == m, REMOVED, cand)
            z = z + jnp.exp(m - m0)
        taken = jnp.where(cand < MARK_LIMIT, 1.0, 0.0)
        rowsum = lambda a, b: jnp.sum(taken[a:b, :], axis=0, keepdims=True)
        lim = jnp.concatenate([rowsum(0, K)] + [rowsum(K + 8 * j, K + 8 * (j + 1)) for j in range(7)]
                              + [taken[K + 56:, :]], axis=0)
        return lim, z, jnp.sum(taken, axis=0, keepdims=True)

    def store_selection(r1, r2, t1, t2, lim, z):
        r1_s[...] = r1
        r2_s[...] = r2
        lim_s[...] = lim
        top_s[0:1, :] = t1[0:1, :]
        top_s[1:2, :] = t2[0:1, :]
        top_s[2:3, :] = z

    def head(h, carry):
        r = pl.multiple_of(h * 2 * PEER_NKEYS, 2 * PEER_NKEYS)
        s1 = _dot(keys_ref[2 * h], qt_s[pl.ds(r, PEER_NKEYS), :].astype(BF16))
        s2 = _dot(keys_ref[2 * h + 1], qt_s[pl.ds(r + PEER_NKEYS, PEER_NKEYS), :].astype(BF16))
        r1, t1, n1 = topk_fast(s1)
        r2, t2, n2 = topk_fast(s2)
        lim, z, n3 = pairs_fast(t1, t2)
        store_selection(r1, r2, t1, t2, lim, z)
        off_count = jnp.abs(n1 - K) + jnp.abs(n2 - K) + jnp.abs(n3 - K)

        @pl.when(jnp.max(off_count) > 0.5)
        def _():
            r1, t1 = topk_exact(s1)
            r2, t2 = topk_exact(s2)
            lim, z = pairs_exact(t1, t2)
            store_selection(r1, r2, t1, t2, lim, z)

        r1 = r1_s[...]
        lim = lim_s[...]
        lim1 = jnp.zeros((PEER_NKEYS, tb), F32)
        for k in range(K):
            lim1 = lim1 + jnp.where(r1 == float(k), lim[k:k + 1, :], 0.0)
        e1 = jnp.exp(s1 - top_s[0:1, :])
        e2 = jnp.exp(s2 - top_s[1:2, :]) / top_s[2:3, :]
        r2 = r2_s[...]
        for ts in range(tb // 128):
            cols = slice(128 * ts, 128 * (ts + 1))
            rank2_ref[h, ts] = pltpu.bitcast(r2[:, cols].astype(BF16), jnp.uint32)
            lim1_ref[h, ts] = lim1[:, cols]
            e1_ref[h, ts] = e1[:, cols]
            e2_ref[h, ts] = pltpu.bitcast(e2[:, cols].astype(BF16), jnp.uint32)
        return carry

    lax.fori_loop(0, PEER_HEADS, head, 0)


def _peer_select(x2, ohg, onsa, wo1, wo2, fg, wqt, keys, tb):
    N, D = x2.shape
    full = lambda a: pl.BlockSpec(a.shape, lambda i: (0,) * a.ndim)
    sel_spec = pl.BlockSpec((PEER_HEADS, tb // 128, PEER_NKEYS, 128), lambda i: (0, i, 0, 0))
    sel_shape = jax.ShapeDtypeStruct((PEER_HEADS, N // 128, PEER_NKEYS, 128), F32)
    packed_spec = pl.BlockSpec((PEER_HEADS, tb // 128, PEER_NKEYS // 2, 128), lambda i: (0, i, 0, 0))
    packed_shape = jax.ShapeDtypeStruct((PEER_HEADS, N // 128, PEER_NKEYS // 2, 128), jnp.uint32)
    return pl.pallas_call(
        _peer_sel_kernel,
        grid=(N // tb,),
        in_specs=[pl.BlockSpec((tb, D), lambda i: (i, 0)), pl.BlockSpec((tb, ohg.shape[1]), lambda i: (i, 0)),
                  pl.BlockSpec((tb, onsa.shape[1]), lambda i: (i, 0)),
                  full(wo1), full(wo2), full(fg), full(wqt), full(keys)],
        out_specs=[pl.BlockSpec((tb, D), lambda i: (i, 0)), pl.BlockSpec((D, tb), lambda i: (0, i)),
                   packed_spec, packed_spec, sel_spec, sel_spec],
        out_shape=[jax.ShapeDtypeStruct((N, D), F32), jax.ShapeDtypeStruct((D, N), BF16),
                   packed_shape, packed_shape, sel_shape, sel_shape],
        scratch_shapes=[pltpu.VMEM((wqt.shape[0], tb), F32), pltpu.VMEM((PEER_NKEYS, tb), F32),
                        pltpu.VMEM((PEER_NKEYS, tb), F32), pltpu.VMEM((PEER_TOPK, tb), F32), pltpu.VMEM((8, tb), F32)],
        compiler_params=_params("parallel"),
        name="peer_select",
    )(x2, ohg, onsa, wo1, wo2, fg, wqt, keys)


def _peer_dense_kernel(xnt_ref, u_ref, vt_ref, rank2_ref, e2_ref, lim1_ref, e1_ref, h1_ref, p_ref,
                       pproj_ref, pgn_ref, pgw_ref, out_ref, acc_ref, at_s, wg_s, *, eb):
    e = pl.program_id(1)

    @pl.when(e == 0)
    def _():
        acc_ref[...] = jnp.zeros_like(acc_ref)

    groups = eb // PEER_NKEYS
    half = PEER_NKEYS // 2
    for j in range(groups):
        i1 = e * groups + j
        rows = slice(PEER_NKEYS * j, PEER_NKEYS * (j + 1))
        at_s[rows, :] = _dot(u_ref[rows, :], xnt_ref[...])
        for ts in range(at_s.shape[1] // 128):
            cols = slice(128 * ts, 128 * (ts + 1))
            wt = None
            for h in range(PEER_HEADS):
                r2 = pltpu.bitcast(rank2_ref[h, half * ts:half * (ts + 1), :], BF16)
                w2 = pltpu.bitcast(e2_ref[h, half * ts:half * (ts + 1), :], BF16)
                lim = jnp.broadcast_to(lim1_ref[h, ts, pl.ds(i1, 1), :].astype(BF16), (PEER_NKEYS, 128))
                w1 = jnp.broadcast_to(e1_ref[h, ts, pl.ds(i1, 1), :].astype(BF16), (PEER_NKEYS, 128))
                w = jnp.where(r2 < lim, w2, 0.0) * w1
                wt = w if wt is None else wt + w
            wg_s[rows, cols] = wt * jax.nn.gelu(at_s[rows, cols]).astype(BF16)
    acc_ref[...] += _dot(vt_ref[...], wg_s[...])

    @pl.when(e == pl.num_programs(1) - 1)
    def _():
        h2 = h1_ref[...] + acc_ref[...].T
        gate = jax.nn.sigmoid(_dot((_rms(h2) * pgn_ref[...]).astype(BF16), pgw_ref[...]))
        out_ref[...] = h2 + _dot(p_ref[...].astype(BF16), pproj_ref[...]) * gate


def _peer_dense(xnt, u, vt, rank2, e2, lim1, e1, h1, p2, pproj, pgn, pgw, tb, eb):
    D, N = xnt.shape
    ne = u.shape[0] // eb
    full = lambda a: pl.BlockSpec(a.shape, lambda t, e: (0,) * a.ndim)
    sel_spec = pl.BlockSpec((PEER_HEADS, tb // 128, PEER_NKEYS, 128), lambda t, e: (0, t, 0, 0))
    packed_spec = pl.BlockSpec((PEER_HEADS, tb // 128 * (PEER_NKEYS // 2), 128), lambda t, e: (0, t, 0))
    rank2 = rank2.reshape(PEER_HEADS, -1, 128)
    e2 = e2.reshape(PEER_HEADS, -1, 128)
    return pl.pallas_call(
        functools.partial(_peer_dense_kernel, eb=eb),
        grid=(N // tb, ne),
        in_specs=[pl.BlockSpec((D, tb), lambda t, e: (0, t)), pl.BlockSpec((eb, D), lambda t, e: (e, 0)),
                  pl.BlockSpec((D, eb), lambda t, e: (0, e)), packed_spec, packed_spec, sel_spec, sel_spec,
                  pl.BlockSpec((tb, D), lambda t, e: (t, 0)), pl.BlockSpec((tb, p2.shape[1]), lambda t, e: (t, 0)),
                  full(pproj), full(pgn), full(pgw)],
        out_specs=pl.BlockSpec((tb, D), lambda t, e: (t, 0)),
        out_shape=jax.ShapeDtypeStruct((N, D), F32),
        scratch_shapes=[pltpu.VMEM((D, tb), F32), pltpu.VMEM((eb, tb), F32), pltpu.VMEM((eb, tb), BF16)],
        compiler_params=_params("parallel", "arbitrary"),
        name="peer_dense",
    )(xnt, u, vt, rank2, e2, lim1, e1, h1, p2, pproj, pgn, pgw)


def _conv_matrix(ncp):
    ratio = SLC_LEN // CMP_STRIDE
    weights = np.convolve(np.ones(ratio), np.ones(CMP_LEN // CMP_STRIDE))
    m = np.zeros((ncp, 128), np.float32)
    for j in range(128):
        for o, w in enumerate(weights):
            if ratio * j + o < ncp:
                m[ratio * j + o, j] = w
    return jnp.asarray(m)


def _block(n, pref):
    while n % pref:
        pref //= 2
    return pref


def kernel(x, p, mix_norm, w_in, hg_lb_logits, hg_out_norm, nsa_q_norm, nsa_k_norm, cmp_pe, cmp_w1, cmp_w2,
           nsa_out_norm, w_out, ffn_norm, peer_wq, peer_keys, peer_u, peer_v, ple_proj, ple_gate_norm, ple_gate_w):
    B, T, D = x.shape
    N = B * T
    x2 = x.reshape(N, D)
    nsa_w = NSA_HEADS * NSA_DH
    kvw = NSA_G * NSA_DH

    w = w_in[0]
    c0 = 4 * HG_WIDTH
    whg = w[:, :c0].astype(BF16)
    wq = w[:, c0:c0 + nsa_w].astype(BF16)
    wkv = w[:, c0 + nsa_w:c0 + nsa_w + 6 * kvw].astype(BF16)
    wg = w[:, c0 + nsa_w + 6 * kvw:]
    per_g = 3 * NSA_R
    wgate = jnp.zeros((D, 256), F32)
    for g in range(NSA_G):
        wgate = wgate.at[:, 128 * g:128 * g + per_g].set(wg[:, per_g * g:per_g * (g + 1)])
    wgate = wgate.astype(BF16)

    hg, qt, kcr, vcr, ks, vst, kw, vwt, gates_t = _inproj(
        x2, mix_norm, whg, wq, wkv, wgate, nsa_q_norm.reshape(NSA_DH, 1), nsa_k_norm[0], B, T, _block(T, 256))

    o_hg = _hgrn(hg.reshape(B, T, 4 * HG_WIDTH), hg_lb_logits, hg_out_norm, _block(T // HG_CHUNK, 4))

    ncp = T // CMP_STRIDE
    stride_rows = lambda a: a.reshape(B * NSA_G, ncp, CMP_STRIDE * NSA_DH)
    pe8 = jnp.broadcast_to(cmp_pe[0].reshape(2, 1, CMP_LEN * NSA_DH), (2, 8, CMP_LEN * NSA_DH))
    kc, vct = _compress(stride_rows(kcr), stride_rows(vcr), cmp_w1[0], cmp_w2[0], pe8, nsa_k_norm[0])
    kc = kc.reshape(B, NSA_G, ncp, NSA_DH)
    vct = vct.reshape(B, NSA_G, NSA_DH, ncp)

    o_nsa = _nsa(qt, gates_t, kc, vct, ks, vst, kw, vwt, _conv_matrix(ncp).T, nsa_out_norm)

    wo = w_out[0].astype(BF16)
    wqt = peer_wq[0].T.astype(BF16)
    keys = peer_keys[0].reshape(2 * PEER_HEADS, PEER_NKEYS, -1).astype(BF16)
    h1, xnt, rank2, e2, lim1, e1 = _peer_select(
        x2, o_hg.reshape(N, HG_WIDTH), o_nsa, wo[:HG_WIDTH], wo[HG_WIDTH:], ffn_norm, wqt, keys, _block(N, 256))

    out = _peer_dense(xnt, peer_u[0].astype(BF16), peer_v[0].T.astype(BF16), rank2, e2, lim1, e1, h1,
                      p[0].reshape(N, -1), ple_proj[0].astype(BF16), ple_gate_norm, ple_gate_w[0].astype(BF16),
                      _block(N, 512), 512)
    return out.reshape(B, T, D)
```

```python
import functools

import numpy as np
import jax
import jax.numpy as jnp
from jax import lax
from jax.experimental import pallas as pl
from jax.experimental.pallas import tpu as pltpu

F32 = jnp.float32
BF16 = jnp.bfloat16
HIGHEST = lax.Precision.HIGHEST
EPS = 1e-6
NEG = -1e30
REMOVED = -3e38
MARK_STEP = 2e36
MARK_LIMIT = -2.5e38

HG_HEADS = 4
HG_D = 128
HG_CHUNK = 64
HG_WIDTH = HG_HEADS * HG_D
NSA_DH = 64
NSA_HEADS = 8
NSA_G = 2
NSA_R = NSA_HEADS // NSA_G
CMP_LEN = 32
CMP_STRIDE = 16
SLC_LEN = 64
SLC_TOPK = 16
WIN = 512
TQ = 128
SLC_CHUNK = 256
PEER_HEADS = 8
PEER_NKEYS = 128
PEER_TOPK = 16
VMEM_LIMIT = 56 * 1024 * 1024


def _dot(a, b, precision=None):
    return lax.dot_general(a, b, (((1,), (0,)), ((), ())), preferred_element_type=F32, precision=precision)


def _dot_nt(a, b):
    return lax.dot_general(a, b, (((1,), (1,)), ((), ())), preferred_element_type=F32)


def _dot_tn(a, b):
    return lax.dot_general(a, b, (((0,), (0,)), ((), ())), preferred_element_type=F32)


def _rms(x):
    return x * lax.rsqrt(jnp.mean(x * x, axis=-1, keepdims=True) + EPS)


def _params(*sem):
    return pltpu.CompilerParams(dimension_semantics=sem, vmem_limit_bytes=VMEM_LIMIT)


def _alibi_slope_row(g, width):
    lane_h = jnp.right_shift(lax.broadcasted_iota(jnp.int32, (1, width), 1), TQ.bit_length() - 1)
    row = jnp.zeros((1, width), F32)
    for h in range(NSA_R):
        row = jnp.where(lane_h == h, jnp.where(g == 0, 2.0 ** -(h + 1), 2.0 ** -(NSA_R + h + 1)).astype(F32), row)
    return row


def _inproj_kernel(x_ref, g_ref, whg_ref, wq_ref, wkv_ref, wgate_ref, qgc_ref, kg_ref,
                   hg_ref, qt_ref, kcr_ref, vcr_ref, ks_ref, vst_ref, kw_ref, vwt_ref, gate_ref, *, nt):
    a = (_rms(x_ref[...]) * g_ref[...]).astype(BF16)
    hg_ref[...] = _dot(a, whg_ref[...])
    tb = a.shape[0]
    nqb = tb // TQ
    width = NSA_R * TQ

    qt = _dot(a, wq_ref[...]).T
    aug_row = lax.broadcasted_iota(jnp.int32, (NSA_DH, width), 0)
    for g in range(NSA_G):
        slope = _alibi_slope_row(g, width)
        aug = jnp.where(aug_row == 0, slope, jnp.where(aug_row == 1, slope * TQ, 0.0)).astype(BF16)
        for j in range(nqb):
            qt_ref[0, g, j, NSA_DH:2 * NSA_DH, :] = aug
        for h in range(NSA_R):
            blk = qt[NSA_DH * (NSA_R * g + h):NSA_DH * (NSA_R * g + h + 1)]
            r = lax.rsqrt(jnp.mean(blk * blk, axis=0, keepdims=True) + EPS)
            qn = (blk * r * qgc_ref[...] * NSA_DH ** -0.5).astype(BF16)
            for j in range(nqb):
                qt_ref[0, g, j, 0:NSA_DH, TQ * h:TQ * (h + 1)] = qn[:, TQ * j:TQ * (j + 1)]

    kv = _dot(a, wkv_ref[...])
    kvt = kv.T
    pos = (pl.program_id(0) % nt) * tb + lax.broadcasted_iota(jnp.int32, (tb, NSA_DH), 0)
    aug_col = lax.broadcasted_iota(jnp.int32, (tb, NSA_DH), 1)
    pos_aug = jnp.where(aug_col == 0, jnp.bitwise_and(pos, TQ - 1),
                        jnp.where(aug_col == 1, jnp.right_shift(pos, TQ.bit_length() - 1), 0)).astype(F32)

    def part(j, g):
        return kv[:, 128 * j + NSA_DH * g:128 * j + NSA_DH * (g + 1)]

    def part_t(j, g):
        return kvt[128 * j + NSA_DH * g:128 * j + NSA_DH * (g + 1)]

    for g in range(NSA_G):
        kcr_ref[0, g] = part(0, g)
        vcr_ref[0, g] = part(1, g)
        ks_ref[0, g] = jnp.concatenate([_rms(part(2, g)) * kg_ref[1:2, :], pos_aug], axis=1).astype(BF16)
        vst_ref[0, g] = part_t(3, g).astype(BF16)
        kw_ref[0, g] = jnp.concatenate([_rms(part(4, g)) * kg_ref[2:3, :], pos_aug], axis=1).astype(BF16)
        vwt_ref[0, g] = part_t(5, g).astype(BF16)
    gate_ref[...] = jax.nn.sigmoid(_dot(a, wgate_ref[...])).T


def _inproj(x2, g, whg, wq, wkv, wgate, qgc, kg, B, T, tb):
    N, D = x2.shape
    nt = T // tb
    nqb = tb // TQ
    full = lambda shape: pl.BlockSpec(shape, lambda i: (0,) * len(shape))
    row_spec = lambda w: pl.BlockSpec((1, NSA_G, tb, w), lambda i: (i // nt, 0, i % nt, 0))
    col_spec = pl.BlockSpec((1, NSA_G, NSA_DH, tb), lambda i: (i // nt, 0, 0, i % nt))
    row_shape = lambda w, dt: jax.ShapeDtypeStruct((B, NSA_G, T, w), dt)
    col_shape = jax.ShapeDtypeStruct((B, NSA_G, NSA_DH, T), BF16)
    return pl.pallas_call(
        functools.partial(_inproj_kernel, nt=nt),
        grid=(N // tb,),
        in_specs=[pl.BlockSpec((tb, D), lambda i: (i, 0)), full(g.shape), full(whg.shape), full(wq.shape),
                  full(wkv.shape), full(wgate.shape), full(qgc.shape), full(kg.shape)],
        out_specs=[pl.BlockSpec((tb, 4 * HG_WIDTH), lambda i: (i, 0)),
                   pl.BlockSpec((1, NSA_G, nqb, 2 * NSA_DH, NSA_R * TQ), lambda i: (i // nt, 0, i % nt, 0, 0)),
                   row_spec(NSA_DH), row_spec(NSA_DH), row_spec(2 * NSA_DH), col_spec, row_spec(2 * NSA_DH), col_spec,
                   pl.BlockSpec((256, tb), lambda i: (0, i))],
        out_shape=[jax.ShapeDtypeStruct((N, 4 * HG_WIDTH), F32),
                   jax.ShapeDtypeStruct((B, NSA_G, T // TQ, 2 * NSA_DH, NSA_R * TQ), BF16),
                   row_shape(NSA_DH, F32), row_shape(NSA_DH, F32), row_shape(2 * NSA_DH, BF16), col_shape,
                   row_shape(2 * NSA_DH, BF16), col_shape,
                   jax.ShapeDtypeStruct((256, N), F32)],
        compiler_params=_params("parallel"),
        name="inproj",
    )(x2, g, whg, wq, wkv, wgate, qgc, kg)


def _hgrn_kernel(hg_ref, lbl_ref, gain_ref, o_ref, st_ref, *, nb, cpb):
    @pl.when(pl.program_id(0) == 0)
    def _():
        st_ref[...] = jnp.zeros_like(st_ref)

    logits = lbl_ref[...]
    ex = jnp.exp(logits - jnp.max(logits, axis=0, keepdims=True))
    lb_all = ex[0:1, :] / jnp.sum(ex, axis=0, keepdims=True)
    C = HG_CHUNK
    row = lax.broadcasted_iota(jnp.int32, (C, C), 0)
    col = lax.broadcasted_iota(jnp.int32, (C, C), 1)
    tril = row >= col
    trilf = jnp.where(tril, 1.0, 0.0).astype(F32)

    def chunk(c, carry):
        r0 = pl.multiple_of(c * C, C)
        rows = pl.ds(r0, C)
        for b in range(nb):
            for h in range(HG_HEADS):
                sl = slice(HG_D * h, HG_D * (h + 1))
                q, f, iv, g = [hg_ref[b, rows, j * HG_WIDTH + HG_D * h:j * HG_WIDTH + HG_D * (h + 1)] for j in range(4)]
                st = st_ref[b * HG_HEADS + h]
                lb = lb_all[:, sl]
                forget = lb + (1.0 - lb) * jax.nn.sigmoid(f)
                logf = jnp.log(forget)
                key = 1.0 - forget
                query = jax.nn.silu(q) * HG_D ** -0.5
                G = _dot(trilf, logf, HIGHEST)
                g_ref = G[C // 2 - 1:C // 2, :]
                g_last = G[C - 1:C, :]
                qa = (query * jnp.exp(G - g_ref)).astype(BF16)
                ka = (key * jnp.exp(g_ref - G)).astype(BF16)
                a = jnp.where(tril, _dot_nt(qa, ka), 0.0)
                vb = iv.astype(BF16)
                o = _dot(a.astype(BF16), vb)
                o = o + _dot_nt((query * jnp.exp(G)).astype(BF16), st.astype(BF16))
                kb = (key * jnp.exp(g_last - G)).astype(BF16)
                st_ref[b * HG_HEADS + h] = st * jnp.exp(g_last) + _dot_tn(vb, kb)
                on = _rms(o) * gain_ref[:, sl] * jax.nn.silu(g)
                o_ref[b, rows, sl] = on.astype(o_ref.dtype)
        return carry

    lax.fori_loop(0, cpb, chunk, 0)


def _hgrn(hg3, lb_logits, gain, cpb):
    B, T, W = hg3.shape
    rows = cpb * HG_CHUNK
    return pl.pallas_call(
        functools.partial(_hgrn_kernel, nb=B, cpb=cpb),
        grid=(T // rows,),
        in_specs=[pl.BlockSpec((B, rows, W), lambda i: (0, i, 0)),
                  pl.BlockSpec(lb_logits.shape, lambda i: (0, 0)),
                  pl.BlockSpec(gain.shape, lambda i: (0, 0))],
        out_specs=pl.BlockSpec((B, rows, HG_WIDTH), lambda i: (0, i, 0)),
        out_shape=jax.ShapeDtypeStruct((B, T, HG_WIDTH), BF16),
        scratch_shapes=[pltpu.VMEM((B * HG_HEADS, HG_D, HG_D), F32)],
        compiler_params=_params("arbitrary"),
        name="hgrn2",
    )(hg3, lb_logits, gain)


def _cmp_kernel(xk_ref, xv_ref, w1_ref, w2_ref, pe_ref, kg_ref, kc_ref, vct_ref):
    half = CMP_STRIDE * NSA_DH
    ys = []
    for idx, x_ref in enumerate((xk_ref, xv_ref)):
        x = x_ref[0]
        n = x.shape[0]
        w1 = w1_ref[idx]
        first = _dot(x, w1[:half], HIGHEST)
        second = _dot(x, w1[half:], HIGHEST)
        const = _dot(pe_ref[idx], w1, HIGHEST)[0:1, :]
        y = jax.nn.gelu(first + pltpu.roll(second, n - 1, 0) + const)
        ys.append(_dot(y, w2_ref[idx], HIGHEST))
    kc_ref[0] = (_rms(ys[0]) * kg_ref[0:1, :]).astype(BF16)
    vct_ref[0] = jnp.concatenate(ys, axis=1).T[NSA_DH:].astype(BF16)


def _compress(xk, xv, w1, w2, pe8, kg):
    BG, n, K = xk.shape
    full = lambda a: pl.BlockSpec(a.shape, lambda i: (0,) * a.ndim)
    blk = pl.BlockSpec((1, n, K), lambda i: (i, 0, 0))
    return pl.pallas_call(
        _cmp_kernel,
        grid=(BG,),
        in_specs=[blk, blk, full(w1), full(w2), full(pe8), full(kg)],
        out_specs=[pl.BlockSpec((1, n, NSA_DH), lambda i: (i, 0, 0)), pl.BlockSpec((1, NSA_DH, n), lambda i: (i, 0, 0))],
        out_shape=[jax.ShapeDtypeStruct((BG, n, NSA_DH), BF16), jax.ShapeDtypeStruct((BG, NSA_DH, n), BF16)],
        compiler_params=_params("parallel"),
        name="nsa_compress",
    )(xk, xv, w1, w2, pe8, kg)


def _nsa_kernel(qt_ref, gate_ref, kc_ref, vct_ref, ks_ref, vst_ref, kw_ref, vwt_ref, mconvt_ref, og_ref,
                o_ref, m_s, l_s, acc_s, sel_s, sa_s, sb_s):
    g = pl.program_id(1)
    i = pl.program_id(2)
    t0 = i * TQ
    width = NSA_R * TQ
    qa = qt_ref[0, 0, 0]
    q = qa[0:NSA_DH]
    slope_row = _alibi_slope_row(g, width)
    hcols = [slice(h * TQ, (h + 1) * TQ) for h in range(NSA_R)]

    ncp = kc_ref.shape[2]
    nn = lax.broadcasted_iota(jnp.int32, (ncp, TQ), 0)
    tt = lax.broadcasted_iota(jnp.int32, (ncp, TQ), 1)
    dist_c = t0 + tt - (nn * CMP_STRIDE + CMP_LEN - 1)
    valid_c = dist_c >= 0
    dist_cf = dist_c.astype(F32)
    s_c = _dot(kc_ref[0, 0], q)
    psum = jnp.zeros((ncp, TQ), F32)
    probs = []
    for h in range(NSA_R):
        s = jnp.where(valid_c, s_c[:, hcols[h]] - slope_row[:, hcols[h]] * dist_cf, NEG)
        m = jnp.max(s, axis=0, keepdims=True)
        m = jnp.where(m > 0.1 * NEG, m, 0.0)
        e = jnp.where(valid_c, jnp.exp(s - m), 0.0)
        p = e / jnp.maximum(jnp.sum(e, axis=0, keepdims=True), 1e-30)
        psum = psum + p
        probs.append(p.astype(BF16))
    o_cmp = _dot(vct_ref[0, 0], jnp.concatenate(probs, axis=1))

    imp = _dot(mconvt_ref[...], psum, HIGHEST)
    blk = lax.broadcasted_iota(jnp.int32, (128, TQ), 0).astype(F32)
    tcol = lax.broadcasted_iota(jnp.int32, (128, TQ), 1)
    cur = jnp.right_shift(t0 + tcol, SLC_LEN.bit_length() - 1).astype(F32)
    forced = (blk == 0.0) | (blk == cur) | (blk == cur - 1.0)
    v = jnp.where(forced, -NEG, jnp.where(blk <= cur, imp, NEG))
    sel = jnp.zeros((128, TQ), F32)
    for _ in range(SLC_TOPK):
        m = jnp.max(v, axis=0, keepdims=True)
        idx = jnp.min(jnp.where(v == m, blk, 128.0), axis=0, keepdims=True)
        pick = blk == idx
        sel = jnp.where(pick, 1.0, sel)
        v = jnp.where(pick, REMOVED, v)
    sel_s[...] = sel

    m_s[...] = jnp.full(m_s.shape, NEG, F32)
    l_s[...] = jnp.zeros(l_s.shape, F32)
    acc_s[...] = jnp.zeros(acc_s.shape, F32)
    keyr = lax.broadcasted_iota(jnp.int32, (SLC_CHUNK, TQ), 0)
    tokc = lax.broadcasted_iota(jnp.int32, (SLC_CHUNK, TQ), 1)
    blocks_per_chunk = SLC_CHUNK // SLC_LEN

    last_k0 = ks_ref.shape[2] - SLC_CHUNK

    def scores(c):
        k0 = pl.multiple_of(jnp.minimum(c * SLC_CHUNK, last_k0), SLC_CHUNK)
        return _dot(ks_ref[0, 0, pl.ds(k0, SLC_CHUNK), :], qa)

    def update(s_ref, c):
        k0 = pl.multiple_of(c * SLC_CHUNK, SLC_CHUNK)
        selx = jnp.concatenate(
            [jnp.broadcast_to(sel_s[pl.ds(blocks_per_chunk * c + j, 1), :], (SLC_LEN, TQ)) for j in range(blocks_per_chunk)],
            axis=0)
        negm = jnp.where(selx > 0.5, jnp.where(k0 + keyr <= t0 + tokc, 0.0, NEG), NEG)
        s = s_ref[...] + jnp.concatenate([negm] * NSA_R, axis=1)
        m_old = m_s[...]
        m_new = jnp.maximum(m_old, jnp.max(s, axis=0, keepdims=True))
        alpha = jnp.exp(m_old - m_new)
        p = jnp.exp(s - m_new)
        l_s[...] = alpha * l_s[...] + jnp.sum(p, axis=0, keepdims=True)
        acc_s[...] = alpha * acc_s[...] + _dot(vst_ref[0, 0, :, pl.ds(k0, SLC_CHUNK)], p.astype(BF16))
        m_s[...] = m_new

    sa_s[...] = scores(0)

    def slc_pair(c, carry):
        sb_s[...] = scores(2 * c + 1)
        update(sa_s, 2 * c)
        sa_s[...] = scores(2 * c + 2)
        update(sb_s, 2 * c + 1)
        return carry

    n_pairs = lax.shift_right_logical(t0 + TQ + 2 * SLC_CHUNK - 1, SLC_CHUNK.bit_length())
    lax.fori_loop(0, n_pairs, slc_pair, 0)
    o_s = acc_s[...] / l_s[...]

    wlen = WIN + TQ
    k0 = pl.multiple_of(jnp.maximum(t0 - WIN, 0), TQ)
    wkey = lax.broadcasted_iota(jnp.int32, (wlen, TQ), 0)
    dist = t0 + lax.broadcasted_iota(jnp.int32, (wlen, TQ), 1) - (k0 + wkey)
    negm = jnp.where(dist >= 0, jnp.where(dist < WIN, 0.0, NEG), NEG)
    s = _dot(kw_ref[0, 0, pl.ds(k0, wlen), :], qa) + jnp.concatenate([negm] * NSA_R, axis=1)
    p = jnp.exp(s - jnp.max(s, axis=0, keepdims=True))
    o_w = _dot(vwt_ref[0, 0, :, pl.ds(k0, wlen)], p.astype(BF16)) / jnp.sum(p, axis=0, keepdims=True)

    gts = gate_ref[...]
    outs = []
    for h in range(NSA_R):
        cols = hcols[h]
        o = (gts[3 * h:3 * h + 1, :] * o_cmp[:, cols] + gts[3 * h + 1:3 * h + 2, :] * o_s[:, cols]
             + gts[3 * h + 2:3 * h + 3, :] * o_w[:, cols])
        outs.append(o * lax.rsqrt(jnp.mean(o * o, axis=0, keepdims=True) + EPS))
    o_ref[...] = (jnp.concatenate(outs, axis=0).T * og_ref[...]).astype(o_ref.dtype)


def _nsa(qt, gates_t, kc, vct, ks, vst, kw, vwt, mconvt, og):
    B, _, nq, _, width = qt.shape
    T = nq * TQ
    ncp = kc.shape[2]
    rows = pl.BlockSpec((1, 1, T, 2 * NSA_DH), lambda b, g, i: (b, g, 0, 0))
    cols = pl.BlockSpec((1, 1, NSA_DH, T), lambda b, g, i: (b, g, 0, 0))
    ow = NSA_R * NSA_DH
    return pl.pallas_call(
        _nsa_kernel,
        grid=(B, NSA_G, nq),
        in_specs=[pl.BlockSpec((1, 1, 1, 2 * NSA_DH, width), lambda b, g, i: (b, g, i, 0, 0)),
                  pl.BlockSpec((128, TQ), lambda b, g, i: (g, b * nq + i)),
                  pl.BlockSpec((1, 1, ncp, NSA_DH), lambda b, g, i: (b, g, 0, 0)),
                  pl.BlockSpec((1, 1, NSA_DH, ncp), lambda b, g, i: (b, g, 0, 0)),
                  rows, cols, rows, cols,
                  pl.BlockSpec(mconvt.shape, lambda b, g, i: (0, 0)),
                  pl.BlockSpec((1, ow), lambda b, g, i: (0, g))],
        out_specs=pl.BlockSpec((TQ, ow), lambda b, g, i: (b * nq + i, g)),
        out_shape=jax.ShapeDtypeStruct((B * T, NSA_G * ow), BF16),
        scratch_shapes=[pltpu.VMEM((1, width), F32), pltpu.VMEM((1, width), F32),
                        pltpu.VMEM((NSA_DH, width), F32), pltpu.VMEM((128, TQ), F32),
                        pltpu.VMEM((SLC_CHUNK, width), F32), pltpu.VMEM((SLC_CHUNK, width), F32)],
        compiler_params=_params("parallel", "parallel", "arbitrary"),
        name="nsa_attn",
    )(qt, gates_t, kc, vct, ks, vst, kw, vwt, mconvt, og)


def _peer_sel_kernel(x_ref, ohg_ref, onsa_ref, wo1_ref, wo2_ref, fg_ref, wqt_ref, keys_ref,
                     h1_ref, xnt_ref, rank2_ref, e2_ref, lim1_ref, e1_ref, qt_s, r1_s, r2_s, lim_s, top_s):
    h1 = x_ref[...] + _dot(ohg_ref[...], wo1_ref[...]) + _dot(onsa_ref[...], wo2_ref[...])
    h1_ref[...] = h1
    xnt = (_rms(h1) * fg_ref[...]).T.astype(BF16)
    xnt_ref[...] = xnt
    qt_s[...] = _dot(wqt_ref[...], xnt)
    tb = xnt.shape[1]
    K = PEER_TOPK
    rowi = lax.broadcasted_iota(jnp.int32, (PEER_NKEYS, tb), 0).astype(F32)
    k1i = lax.broadcasted_iota(jnp.int32, (K, tb), 0).astype(F32)
    r = lax.broadcasted_iota(jnp.int32, (K + 7 * 8 + 8, tb), 0)
    ci = jnp.where(r < K, r, jnp.where(r < K + 56, K * (1 + jnp.right_shift(r - K, 3)) + jnp.bitwise_and(r - K, 7),
                                       K * (r - (K + 56) + 8))).astype(F32)

    mark = [REMOVED + k * MARK_STEP for k in range(K)]

    def topk_exact(s):
        rank = jnp.full(s.shape, float(PEER_NKEYS), F32)
        tops = []
        for k in range(K):
            m = jnp.max(s, axis=0, keepdims=True)
            idx = jnp.min(jnp.where(s == m, rowi, float(PEER_NKEYS)), axis=0, keepdims=True)
            pick = rowi == idx
            rank = jnp.where(pick, float(k), rank)
            s = jnp.where(pick, REMOVED, s)
            tops.append(m)
        return rank, jnp.concatenate(tops, axis=0)

    def topk_fast(s):
        tops = []
        for k in range(K):
            m = jnp.max(s, axis=0, keepdims=True)
            s = jnp.where(s == m, mark[k], s)
            tops.append(m)
        taken = s < MARK_LIMIT
        rank = jnp.where(taken, jnp.floor((s - mark[0]) * (1.0 / MARK_STEP) + 0.5), float(PEER_NKEYS))
        count = jnp.sum(jnp.where(taken, 1.0, 0.0), axis=0, keepdims=True)
        return rank, jnp.concatenate(tops, axis=0), count

    def candidates(t1, t2):
        return jnp.concatenate([t1[0:1, :] + t2] + [t1[k:k + 1, :] + t2[0:8, :] for k in range(1, 8)]
                               + [t1[8:K, :] + t2[0:1, :]], axis=0)

    def pairs_exact(t1, t2):
        cand = candidates(t1, t2)
        m0 = t1[0:1, :] + t2[0:1, :]
        lim = jnp.zeros((K, tb), F32)
        z = jnp.zeros((1, tb), F32)
        for k in range(K):
            m = jnp.max(cand, axis=0, keepdims=True)
            idx = jnp.min(jnp.where(cand == m, ci, float(K * K)), axis=0, keepdims=True)
            cand = jnp.where(ci == idx, REMOVED, cand)
            lim = lim + jnp.where(k1i == jnp.floor(idx * (1.0 / K)), 1.0, 0.0)
            z = z + jnp.exp(m - m0)
        return lim, z

    def pairs_fast(t1, t2):
        cand = candidates(t1, t2)
        m0 = t1[0:1, :] + t2[0:1, :]
        z = jnp.zeros((1, tb), F32)
        for k in range(K):
            m = jnp.max(cand, axis=0, keepdims=True)
            cand = jnp.where(cand == m, REMOVED, cand)
            z = z + jnp.exp(m - m0)
        taken = jnp.where(cand < MARK_LIMIT, 1.0, 0.0)
        rowsum = lambda a, b: jnp.sum(taken[a:b, :], axis=0, keepdims=True)
        lim = jnp.concatenate([rowsum(0, K)] + [rowsum(K + 8 * j, K + 8 * (j + 1)) for j in range(7)]
                              + [taken[K + 56:, :]], axis=0)
        return lim, z, jnp.sum(taken, axis=0, keepdims=True)

    def store_selection(r1, r2, t1, t2, lim, z):
        r1_s[...] = r1
        r2_s[...] = r2
        lim_s[...] = lim
        top_s[0:1, :] = t1[0:1, :]
        top_s[1:2, :] = t2[0:1, :]
        top_s[2:3, :] = z

    def head(h, carry):
        r = pl.multiple_of(h * 2 * PEER_NKEYS, 2 * PEER_NKEYS)
        s1 = _dot(keys_ref[2 * h], qt_s[pl.ds(r, PEER_NKEYS), :].astype(BF16))
        s2 = _dot(keys_ref[2 * h + 1], qt_s[pl.ds(r + PEER_NKEYS, PEER_NKEYS), :].astype(BF16))
        r1, t1, n1 = topk_fast(s1)
        r2, t2, n2 = topk_fast(s2)
        lim, z, n3 = pairs_fast(t1, t2)
        store_selection(r1, r2, t1, t2, lim, z)
        off_count = jnp.abs(n1 - K) + jnp.abs(n2 - K) + jnp.abs(n3 - K)

        @pl.when(jnp.max(off_count) > 0.5)
        def _():
            r1, t1 = topk_exact(s1)
            r2, t2 = topk_exact(s2)
            lim, z = pairs_exact(t1, t2)
            store_selection(r1, r2, t1, t2, lim, z)

        r1 = r1_s[...]
        lim = lim_s[...]
        lim1 = jnp.zeros((PEER_NKEYS, tb), F32)
        for k in range(K):
            lim1 = lim1 + jnp.where(r1 == float(k), lim[k:k + 1, :], 0.0)
        e1 = jnp.exp(s1 - top_s[0:1, :])
        e2 = jnp.exp(s2 - top_s[1:2, :]) / top_s[2:3, :]
        r2 = r2_s[...]
        for ts in range(tb // 128):
            cols = slice(128 * ts, 128 * (ts + 1))
            rank2_ref[h, ts] = pltpu.bitcast(r2[:, cols].astype(BF16), jnp.uint32)
            lim1_ref[h, ts] = lim1[:, cols]
            e1_ref[h, ts] = e1[:, cols]
            e2_ref[h, ts] = pltpu.bitcast(e2[:, cols].astype(BF16), jnp.uint32)
        return carry

    lax.fori_loop(0, PEER_HEADS, head, 0)


def _peer_select(x2, ohg, onsa, wo1, wo2, fg, wqt, keys, tb):
    N, D = x2.shape
    full = lambda a: pl.BlockSpec(a.shape, lambda i: (0,) * a.ndim)
    sel_spec = pl.BlockSpec((PEER_HEADS, tb // 128, PEER_NKEYS, 128), lambda i: (0, i, 0, 0))
    sel_shape = jax.ShapeDtypeStruct((PEER_HEADS, N // 128, PEER_NKEYS, 128), F32)
    packed_spec = pl.BlockSpec((PEER_HEADS, tb // 128, PEER_NKEYS // 2, 128), lambda i: (0, i, 0, 0))
    packed_shape = jax.ShapeDtypeStruct((PEER_HEADS, N // 128, PEER_NKEYS // 2, 128), jnp.uint32)
    return pl.pallas_call(
        _peer_sel_kernel,
        grid=(N // tb,),
        in_specs=[pl.BlockSpec((tb, D), lambda i: (i, 0)), pl.BlockSpec((tb, ohg.shape[1]), lambda i: (i, 0)),
                  pl.BlockSpec((tb, onsa.shape[1]), lambda i: (i, 0)),
                  full(wo1), full(wo2), full(fg), full(wqt), full(keys)],
        out_specs=[pl.BlockSpec((tb, D), lambda i: (i, 0)), pl.BlockSpec((D, tb), lambda i: (0, i)),
                   packed_spec, packed_spec, sel_spec, sel_spec],
        out_shape=[jax.ShapeDtypeStruct((N, D), F32), jax.ShapeDtypeStruct((D, N), BF16),
                   packed_shape, packed_shape, sel_shape, sel_shape],
        scratch_shapes=[pltpu.VMEM((wqt.shape[0], tb), F32), pltpu.VMEM((PEER_NKEYS, tb), F32),
                        pltpu.VMEM((PEER_NKEYS, tb), F32), pltpu.VMEM((PEER_TOPK, tb), F32), pltpu.VMEM((8, tb), F32)],
        compiler_params=_params("parallel"),
        name="peer_select",
    )(x2, ohg, onsa, wo1, wo2, fg, wqt, keys)


def _peer_dense_kernel(xnt_ref, u_hbm, vt_hbm, rank2_ref, e2_ref, lim1_ref, e1_ref, h1_ref, p_ref,
                       pproj_ref, pgn_ref, pgw_ref, out_ref,
                       acc_ref, at0_s, at1_s, wg0_s, wg1_s, u0_s, u1_s, v0_s, v1_s, xs_s, r2_s, e2_s, lim_s, e1_s,
                       u_sem, v_sem, *, eb, ne):
    e = pl.program_id(1)
    at_s, wg_s, u_s, v_s = (at0_s, at1_s), (wg0_s, wg1_s), (u0_s, u1_s), (v0_s, v1_s)

    def u_copy(block, slot):
        return pltpu.make_async_copy(u_hbm.at[pl.ds(pl.multiple_of(block * eb, eb), eb), :], u_s[slot], u_sem.at[slot])

    def v_copy(block, slot):
        return pltpu.make_async_copy(vt_hbm.at[:, pl.ds(pl.multiple_of(block * eb, eb), eb)], v_s[slot], v_sem.at[slot])

    @pl.when(e == 0)
    def _():
        u_copy(0, 0).start()
        acc_ref[...] = jnp.zeros_like(acc_ref)
        for buf in (at0_s, at1_s, wg0_s, wg1_s, v0_s, v1_s):
            buf[...] = jnp.zeros_like(buf)
        xs_s[...] = xnt_ref[...]
        r2_s[...] = rank2_ref[...]
        e2_s[...] = e2_ref[...]
        lim_s[...] = lim1_ref[...]
        e1_s[...] = e1_ref[...]

    for par in range(2):
        @pl.when((e % 2 == par) & (e < ne))
        def _(par=par):
            u_copy(e, par).wait()

        @pl.when((e % 2 == par) & (e >= 2))
        def _(par=par):
            v_copy(e - 2, par).wait()

        @pl.when((e % 2 == par) & (e + 1 < ne))
        def _(par=par):
            u_copy(e + 1, 1 - par).start()

        @pl.when((e % 2 == par) & (e >= 1) & (e <= ne))
        def _(par=par):
            v_copy(e - 1, 1 - par).start()

    groups = eb // PEER_NKEYS
    half = PEER_NKEYS // 2
    tb = xs_s.shape[1]
    blk_b = jnp.clip(e - 1, 0, ne - 1)

    def weighting(par, ts):
        cols = slice(128 * ts, 128 * (ts + 1))
        for j in range(groups):
            i1 = blk_b * groups + j
            rows = slice(PEER_NKEYS * j, PEER_NKEYS * (j + 1))
            wt = None
            for h in range(PEER_HEADS):
                r2 = pltpu.bitcast(r2_s[h, half * ts:half * (ts + 1), :], BF16)
                w2 = pltpu.bitcast(e2_s[h, half * ts:half * (ts + 1), :], BF16)
                lim = jnp.broadcast_to(lim_s[h, ts, pl.ds(i1, 1), :].astype(BF16), (PEER_NKEYS, 128))
                w1 = jnp.broadcast_to(e1_s[h, ts, pl.ds(i1, 1), :].astype(BF16), (PEER_NKEYS, 128))
                w = jnp.where(r2 < lim, w2, 0.0) * w1
                wt = w if wt is None else wt + w
            wg_s[1 - par][rows, cols] = wt * jax.nn.gelu(at_s[1 - par][rows, cols]).astype(BF16)

    def stages(par):
        nq = tb // 128
        hw = tb // 2
        for q in range(nq):
            cols = slice(hw * (q % 2), hw * (q % 2 + 1))
            if q < nq // 2:
                at_s[par][:, cols] = _dot(u_s[par][...], xs_s[:, cols])
            else:
                acc_ref[:, cols] += _dot(v_s[par][...], wg_s[par][:, cols])
            weighting(par, q)

    for par in range(2):
        @pl.when(e % 2 == par)
        def _(par=par):
            stages(par)

    @pl.when(e == ne + 1)
    def _():
        h2 = h1_ref[...] + acc_ref[...].T
        gate = jax.nn.sigmoid(_dot((_rms(h2) * pgn_ref[...]).astype(BF16), pgw_ref[...]))
        out_ref[...] = h2 + _dot(p_ref[...].astype(BF16), pproj_ref[...]) * gate


def _peer_dense(xnt, u, vt, rank2, e2, lim1, e1, h1, p2, pproj, pgn, pgw, tb, eb):
    D, N = xnt.shape
    ne = u.shape[0] // eb
    full = lambda a: pl.BlockSpec(a.shape, lambda t, e: (0,) * a.ndim)
    sel_block = (PEER_HEADS, tb // 128, PEER_NKEYS, 128)
    sel_spec = pl.BlockSpec(sel_block, lambda t, e: (0, t, 0, 0))
    packed_block = (PEER_HEADS, tb // 128 * (PEER_NKEYS // 2), 128)
    packed_spec = pl.BlockSpec(packed_block, lambda t, e: (0, t, 0))
    rank2 = rank2.reshape(PEER_HEADS, -1, 128)
    e2 = e2.reshape(PEER_HEADS, -1, 128)
    any_spec = pl.BlockSpec(memory_space=pl.ANY)
    return pl.pallas_call(
        functools.partial(_peer_dense_kernel, eb=eb, ne=ne),
        grid=(N // tb, ne + 2),
        in_specs=[pl.BlockSpec((D, tb), lambda t, e: (0, t)), any_spec, any_spec,
                  packed_spec, packed_spec, sel_spec, sel_spec,
                  pl.BlockSpec((tb, D), lambda t, e: (t, 0)), pl.BlockSpec((tb, p2.shape[1]), lambda t, e: (t, 0)),
                  full(pproj), full(pgn), full(pgw)],
        out_specs=pl.BlockSpec((tb, D), lambda t, e: (t, 0)),
        out_shape=jax.ShapeDtypeStruct((N, D), F32),
        scratch_shapes=[pltpu.VMEM((D, tb), F32),
                        pltpu.VMEM((eb, tb), F32), pltpu.VMEM((eb, tb), F32),
                        pltpu.VMEM((eb, tb), BF16), pltpu.VMEM((eb, tb), BF16),
                        pltpu.VMEM((eb, D), BF16), pltpu.VMEM((eb, D), BF16),
                        pltpu.VMEM((D, eb), BF16), pltpu.VMEM((D, eb), BF16),
                        pltpu.VMEM((D, tb), BF16),
                        pltpu.VMEM(packed_block, jnp.uint32), pltpu.VMEM(packed_block, jnp.uint32),
                        pltpu.VMEM(sel_block, F32), pltpu.VMEM(sel_block, F32),
                        pltpu.SemaphoreType.DMA((2,)), pltpu.SemaphoreType.DMA((2,))],
        compiler_params=_params("arbitrary", "arbitrary"),
        name="peer_dense",
    )(xnt, u, vt, rank2, e2, lim1, e1, h1, p2, pproj, pgn, pgw)


def _conv_matrix(ncp):
    ratio = SLC_LEN // CMP_STRIDE
    weights = np.convolve(np.ones(ratio), np.ones(CMP_LEN // CMP_STRIDE))
    m = np.zeros((ncp, 128), np.float32)
    for j in range(128):
        for o, w in enumerate(weights):
            if ratio * j + o < ncp:
                m[ratio * j + o, j] = w
    return jnp.asarray(m)


def _block(n, pref):
    while n % pref:
        pref //= 2
    return pref


def kernel(x, p, mix_norm, w_in, hg_lb_logits, hg_out_norm, nsa_q_norm, nsa_k_norm, cmp_pe, cmp_w1, cmp_w2,
           nsa_out_norm, w_out, ffn_norm, peer_wq, peer_keys, peer_u, peer_v, ple_proj, ple_gate_norm, ple_gate_w):
    B, T, D = x.shape
    N = B * T
    x2 = x.reshape(N, D)
    nsa_w = NSA_HEADS * NSA_DH
    kvw = NSA_G * NSA_DH

    w = w_in[0]
    c0 = 4 * HG_WIDTH
    whg = w[:, :c0].astype(BF16)
    wq = w[:, c0:c0 + nsa_w].astype(BF16)
    wkv = w[:, c0 + nsa_w:c0 + nsa_w + 6 * kvw].astype(BF16)
    wg = w[:, c0 + nsa_w + 6 * kvw:]
    per_g = 3 * NSA_R
    wgate = jnp.zeros((D, 256), F32)
    for g in range(NSA_G):
        wgate = wgate.at[:, 128 * g:128 * g + per_g].set(wg[:, per_g * g:per_g * (g + 1)])
    wgate = wgate.astype(BF16)

    hg, qt, kcr, vcr, ks, vst, kw, vwt, gates_t = _inproj(
        x2, mix_norm, whg, wq, wkv, wgate, nsa_q_norm.reshape(NSA_DH, 1), nsa_k_norm[0], B, T, _block(T, 256))

    o_hg = _hgrn(hg.reshape(B, T, 4 * HG_WIDTH), hg_lb_logits, hg_out_norm, _block(T // HG_CHUNK, 4))

    ncp = T // CMP_STRIDE
    stride_rows = lambda a: a.reshape(B * NSA_G, ncp, CMP_STRIDE * NSA_DH)
    pe8 = jnp.broadcast_to(cmp_pe[0].reshape(2, 1, CMP_LEN * NSA_DH), (2, 8, CMP_LEN * NSA_DH))
    kc, vct = _compress(stride_rows(kcr), stride_rows(vcr), cmp_w1[0], cmp_w2[0], pe8, nsa_k_norm[0])
    kc = kc.reshape(B, NSA_G, ncp, NSA_DH)
    vct = vct.reshape(B, NSA_G, NSA_DH, ncp)

    o_nsa = _nsa(qt, gates_t, kc, vct, ks, vst, kw, vwt, _conv_matrix(ncp).T, nsa_out_norm)

    wo = w_out[0].astype(BF16)
    wqt = peer_wq[0].T.astype(BF16)
    keys = peer_keys[0].reshape(2 * PEER_HEADS, PEER_NKEYS, -1).astype(BF16)
    h1, xnt, rank2, e2, lim1, e1 = _peer_select(
        x2, o_hg.reshape(N, HG_WIDTH), o_nsa, wo[:HG_WIDTH], wo[HG_WIDTH:], ffn_norm, wqt, keys, _block(N, 256))

    out = _peer_dense(xnt, peer_u[0].astype(BF16), peer_v[0].T.astype(BF16), rank2, e2, lim1, e1, h1,
                      p[0].reshape(N, -1), ple_proj[0].astype(BF16), ple_gate_norm, ple_gate_w[0].astype(BF16),
                      _block(N, 512), 512)
    return out.reshape(B, T, D)
```

```python
import functools

import numpy as np
import jax
import jax.numpy as jnp
from jax import lax
from jax.experimental import pallas as pl
from jax.experimental.pallas import tpu as pltpu

F32 = jnp.float32
BF16 = jnp.bfloat16
HIGHEST = lax.Precision.HIGHEST
EPS = 1e-6
NEG = -1e30
REMOVED = -3e38
MARK_STEP = 2e36
MARK_LIMIT = -2.5e38

HG_HEADS = 4
HG_D = 128
HG_CHUNK = 64
HG_WIDTH = HG_HEADS * HG_D
NSA_DH = 64
NSA_HEADS = 8
NSA_G = 2
NSA_R = NSA_HEADS // NSA_G
CMP_LEN = 32
CMP_STRIDE = 16
SLC_LEN = 64
SLC_TOPK = 16
WIN = 512
TQ = 128
SLC_CHUNK = 256
PEER_HEADS = 8
PEER_NKEYS = 128
PEER_TOPK = 16
VMEM_LIMIT = 56 * 1024 * 1024


def _dot(a, b, precision=None):
    return lax.dot_general(a, b, (((1,), (0,)), ((), ())), preferred_element_type=F32, precision=precision)


def _dot_nt(a, b):
    return lax.dot_general(a, b, (((1,), (1,)), ((), ())), preferred_element_type=F32)


def _dot_tn(a, b):
    return lax.dot_general(a, b, (((0,), (0,)), ((), ())), preferred_element_type=F32)


def _rms(x):
    return x * lax.rsqrt(jnp.mean(x * x, axis=-1, keepdims=True) + EPS)


def _params(*sem):
    return pltpu.CompilerParams(dimension_semantics=sem, vmem_limit_bytes=VMEM_LIMIT)


def _alibi_slope_row(g, width):
    lane_h = jnp.right_shift(lax.broadcasted_iota(jnp.int32, (1, width), 1), TQ.bit_length() - 1)
    row = jnp.zeros((1, width), F32)
    for h in range(NSA_R):
        row = jnp.where(lane_h == h, jnp.where(g == 0, 2.0 ** -(h + 1), 2.0 ** -(NSA_R + h + 1)).astype(F32), row)
    return row


def _inproj_kernel(x_ref, g_ref, whg_ref, wq_ref, wkv_ref, wgate_ref, qgc_ref, kg_ref,
                   hg_ref, qt_ref, kcr_ref, vcr_ref, ks_ref, vst_ref, kw_ref, vwt_ref, gate_ref, *, nt):
    a = (_rms(x_ref[...]) * g_ref[...]).astype(BF16)
    hg_ref[...] = _dot(a, whg_ref[...])
    tb = a.shape[0]
    nqb = tb // TQ
    width = NSA_R * TQ

    qt = _dot(a, wq_ref[...]).T
    aug_row = lax.broadcasted_iota(jnp.int32, (NSA_DH, width), 0)
    for g in range(NSA_G):
        slope = _alibi_slope_row(g, width)
        aug = jnp.where(aug_row == 0, slope, jnp.where(aug_row == 1, slope * TQ, 0.0)).astype(BF16)
        for j in range(nqb):
            qt_ref[0, g, j, NSA_DH:2 * NSA_DH, :] = aug
        for h in range(NSA_R):
            blk = qt[NSA_DH * (NSA_R * g + h):NSA_DH * (NSA_R * g + h + 1)]
            r = lax.rsqrt(jnp.mean(blk * blk, axis=0, keepdims=True) + EPS)
            qn = (blk * r * qgc_ref[...] * NSA_DH ** -0.5).astype(BF16)
            for j in range(nqb):
                qt_ref[0, g, j, 0:NSA_DH, TQ * h:TQ * (h + 1)] = qn[:, TQ * j:TQ * (j + 1)]

    kv = _dot(a, wkv_ref[...])
    kvt = kv.T
    pos = (pl.program_id(0) % nt) * tb + lax.broadcasted_iota(jnp.int32, (tb, NSA_DH), 0)
    aug_col = lax.broadcasted_iota(jnp.int32, (tb, NSA_DH), 1)
    pos_aug = jnp.where(aug_col == 0, jnp.bitwise_and(pos, TQ - 1),
                        jnp.where(aug_col == 1, jnp.right_shift(pos, TQ.bit_length() - 1), 0)).astype(F32)

    def part(j, g):
        return kv[:, 128 * j + NSA_DH * g:128 * j + NSA_DH * (g + 1)]

    def part_t(j, g):
        return kvt[128 * j + NSA_DH * g:128 * j + NSA_DH * (g + 1)]

    for g in range(NSA_G):
        kcr_ref[0, g] = part(0, g)
        vcr_ref[0, g] = part(1, g)
        ks_ref[0, g] = jnp.concatenate([_rms(part(2, g)) * kg_ref[1:2, :], pos_aug], axis=1).astype(BF16)
        vst_ref[0, g] = part_t(3, g).astype(BF16)
        kw_ref[0, g] = jnp.concatenate([_rms(part(4, g)) * kg_ref[2:3, :], pos_aug], axis=1).astype(BF16)
        vwt_ref[0, g] = part_t(5, g).astype(BF16)
    gate_ref[...] = jax.nn.sigmoid(_dot(a, wgate_ref[...])).T


def _inproj(x2, g, whg, wq, wkv, wgate, qgc, kg, B, T, tb):
    N, D = x2.shape
    nt = T // tb
    nqb = tb // TQ
    full = lambda shape: pl.BlockSpec(shape, lambda i: (0,) * len(shape))
    row_spec = lambda w: pl.BlockSpec((1, NSA_G, tb, w), lambda i: (i // nt, 0, i % nt, 0))
    col_spec = pl.BlockSpec((1, NSA_G, NSA_DH, tb), lambda i: (i // nt, 0, 0, i % nt))
    row_shape = lambda w, dt: jax.ShapeDtypeStruct((B, NSA_G, T, w), dt)
    col_shape = jax.ShapeDtypeStruct((B, NSA_G, NSA_DH, T), BF16)
    return pl.pallas_call(
        functools.partial(_inproj_kernel, nt=nt),
        grid=(N // tb,),
        in_specs=[pl.BlockSpec((tb, D), lambda i: (i, 0)), full(g.shape), full(whg.shape), full(wq.shape),
                  full(wkv.shape), full(wgate.shape), full(qgc.shape), full(kg.shape)],
        out_specs=[pl.BlockSpec((tb, 4 * HG_WIDTH), lambda i: (i, 0)),
                   pl.BlockSpec((1, NSA_G, nqb, 2 * NSA_DH, NSA_R * TQ), lambda i: (i // nt, 0, i % nt, 0, 0)),
                   row_spec(NSA_DH), row_spec(NSA_DH), row_spec(2 * NSA_DH), col_spec, row_spec(2 * NSA_DH), col_spec,
                   pl.BlockSpec((256, tb), lambda i: (0, i))],
        out_shape=[jax.ShapeDtypeStruct((N, 4 * HG_WIDTH), F32),
                   jax.ShapeDtypeStruct((B, NSA_G, T // TQ, 2 * NSA_DH, NSA_R * TQ), BF16),
                   row_shape(NSA_DH, F32), row_shape(NSA_DH, F32), row_shape(2 * NSA_DH, BF16), col_shape,
                   row_shape(2 * NSA_DH, BF16), col_shape,
                   jax.ShapeDtypeStruct((256, N), F32)],
        compiler_params=_params("parallel"),
        name="inproj",
    )(x2, g, whg, wq, wkv, wgate, qgc, kg)


def _hgrn_kernel(hg_ref, lbl_ref, gain_ref, o_ref, st_ref, *, nb, cpb):
    @pl.when(pl.program_id(0) == 0)
    def _():
        st_ref[...] = jnp.zeros_like(st_ref)

    logits = lbl_ref[...]
    ex = jnp.exp(logits - jnp.max(logits, axis=0, keepdims=True))
    lb_all = ex[0:1, :] / jnp.sum(ex, axis=0, keepdims=True)
    C = HG_CHUNK
    row = lax.broadcasted_iota(jnp.int32, (C, C), 0)
    col = lax.broadcasted_iota(jnp.int32, (C, C), 1)
    tril = row >= col
    trilf = jnp.where(tril, 1.0, 0.0).astype(F32)

    def chunk(c, carry):
        r0 = pl.multiple_of(c * C, C)
        rows = pl.ds(r0, C)
        for b in range(nb):
            for h in range(HG_HEADS):
                sl = slice(HG_D * h, HG_D * (h + 1))
                q, f, iv, g = [hg_ref[b, rows, j * HG_WIDTH + HG_D * h:j * HG_WIDTH + HG_D * (h + 1)] for j in range(4)]
                st = st_ref[b * HG_HEADS + h]
                lb = lb_all[:, sl]
                forget = lb + (1.0 - lb) * jax.nn.sigmoid(f)
                logf = jnp.log(forget)
                key = 1.0 - forget
                query = jax.nn.silu(q) * HG_D ** -0.5
                G = _dot(trilf, logf, HIGHEST)
                g_ref = G[C // 2 - 1:C // 2, :]
                g_last = G[C - 1:C, :]
                qa = (query * jnp.exp(G - g_ref)).astype(BF16)
                ka = (key * jnp.exp(g_ref - G)).astype(BF16)
                a = jnp.where(tril, _dot_nt(qa, ka), 0.0)
                vb = iv.astype(BF16)
                o = _dot(a.astype(BF16), vb)
                o = o + _dot_nt((query * jnp.exp(G)).astype(BF16), st.astype(BF16))
                kb = (key * jnp.exp(g_last - G)).astype(BF16)
                st_ref[b * HG_HEADS + h] = st * jnp.exp(g_last) + _dot_tn(vb, kb)
                on = _rms(o) * gain_ref[:, sl] * jax.nn.silu(g)
                o_ref[b, rows, sl] = on.astype(o_ref.dtype)
        return carry

    lax.fori_loop(0, cpb, chunk, 0)


def _hgrn(hg3, lb_logits, gain, cpb):
    B, T, W = hg3.shape
    rows = cpb * HG_CHUNK
    return pl.pallas_call(
        functools.partial(_hgrn_kernel, nb=B, cpb=cpb),
        grid=(T // rows,),
        in_specs=[pl.BlockSpec((B, rows, W), lambda i: (0, i, 0)),
                  pl.BlockSpec(lb_logits.shape, lambda i: (0, 0)),
                  pl.BlockSpec(gain.shape, lambda i: (0, 0))],
        out_specs=pl.BlockSpec((B, rows, HG_WIDTH), lambda i: (0, i, 0)),
        out_shape=jax.ShapeDtypeStruct((B, T, HG_WIDTH), BF16),
        scratch_shapes=[pltpu.VMEM((B * HG_HEADS, HG_D, HG_D), F32)],
        compiler_params=_params("arbitrary"),
        name="hgrn2",
    )(hg3, lb_logits, gain)


def _cmp_kernel(xk_ref, xv_ref, w1_ref, w2_ref, pe_ref, kg_ref, kc_ref, vct_ref):
    half = CMP_STRIDE * NSA_DH
    ys = []
    for idx, x_ref in enumerate((xk_ref, xv_ref)):
        x = x_ref[0]
        n = x.shape[0]
        w1 = w1_ref[idx]
        first = _dot(x, w1[:half], HIGHEST)
        second = _dot(x, w1[half:], HIGHEST)
        const = _dot(pe_ref[idx], w1, HIGHEST)[0:1, :]
        y = jax.nn.gelu(first + pltpu.roll(second, n - 1, 0) + const)
        ys.append(_dot(y, w2_ref[idx], HIGHEST))
    kc_ref[0] = (_rms(ys[0]) * kg_ref[0:1, :]).astype(BF16)
    vct_ref[0] = jnp.concatenate(ys, axis=1).T[NSA_DH:].astype(BF16)


def _compress(xk, xv, w1, w2, pe8, kg):
    BG, n, K = xk.shape
    full = lambda a: pl.BlockSpec(a.shape, lambda i: (0,) * a.ndim)
    blk = pl.BlockSpec((1, n, K), lambda i: (i, 0, 0))
    return pl.pallas_call(
        _cmp_kernel,
        grid=(BG,),
        in_specs=[blk, blk, full(w1), full(w2), full(pe8), full(kg)],
        out_specs=[pl.BlockSpec((1, n, NSA_DH), lambda i: (i, 0, 0)), pl.BlockSpec((1, NSA_DH, n), lambda i: (i, 0, 0))],
        out_shape=[jax.ShapeDtypeStruct((BG, n, NSA_DH), BF16), jax.ShapeDtypeStruct((BG, NSA_DH, n), BF16)],
        compiler_params=_params("parallel"),
        name="nsa_compress",
    )(xk, xv, w1, w2, pe8, kg)


def _nsa_kernel(qt_ref, gate_ref, kc_ref, vct_ref, ks_ref, vst_ref, kw_ref, vwt_ref, mconvt_ref, og_ref,
                o_ref, m_s, l_s, acc_s, sel_s, sa_s, sb_s, pa_s, pb_s, aa_s, ab_s):
    g = pl.program_id(1)
    i = pl.program_id(2)
    t0 = i * TQ
    width = NSA_R * TQ
    qa = qt_ref[0, 0, 0]
    q = qa[0:NSA_DH]
    slope_row = _alibi_slope_row(g, width)
    hcols = [slice(h * TQ, (h + 1) * TQ) for h in range(NSA_R)]

    ncp = kc_ref.shape[2]
    nn = lax.broadcasted_iota(jnp.int32, (ncp, TQ), 0)
    tt = lax.broadcasted_iota(jnp.int32, (ncp, TQ), 1)
    dist_c = t0 + tt - (nn * CMP_STRIDE + CMP_LEN - 1)
    valid_c = dist_c >= 0
    dist_cf = dist_c.astype(F32)
    s_c = _dot(kc_ref[0, 0], q)
    psum = jnp.zeros((ncp, TQ), F32)
    probs = []
    for h in range(NSA_R):
        s = jnp.where(valid_c, s_c[:, hcols[h]] - slope_row[:, hcols[h]] * dist_cf, NEG)
        m = jnp.max(s, axis=0, keepdims=True)
        m = jnp.where(m > 0.1 * NEG, m, 0.0)
        e = jnp.where(valid_c, jnp.exp(s - m), 0.0)
        p = e / jnp.maximum(jnp.sum(e, axis=0, keepdims=True), 1e-30)
        psum = psum + p
        probs.append(p.astype(BF16))
    o_cmp = _dot(vct_ref[0, 0], jnp.concatenate(probs, axis=1))

    imp = _dot(mconvt_ref[...], psum, HIGHEST)
    blk = lax.broadcasted_iota(jnp.int32, (128, TQ), 0).astype(F32)
    tcol = lax.broadcasted_iota(jnp.int32, (128, TQ), 1)
    cur = jnp.right_shift(t0 + tcol, SLC_LEN.bit_length() - 1).astype(F32)
    forced = (blk == 0.0) | (blk == cur) | (blk == cur - 1.0)
    v = jnp.where(forced, -NEG, jnp.where(blk <= cur, imp, NEG))
    sel = jnp.zeros((128, TQ), F32)
    for _ in range(SLC_TOPK):
        m = jnp.max(v, axis=0, keepdims=True)
        idx = jnp.min(jnp.where(v == m, blk, 128.0), axis=0, keepdims=True)
        pick = blk == idx
        sel = jnp.where(pick, 1.0, sel)
        v = jnp.where(pick, REMOVED, v)
    sel_s[...] = sel

    m_s[...] = jnp.full(m_s.shape, NEG, F32)
    l_s[...] = jnp.zeros(l_s.shape, F32)
    acc_s[...] = jnp.zeros(acc_s.shape, F32)
    keyr = lax.broadcasted_iota(jnp.int32, (SLC_CHUNK, TQ), 0)
    tokc = lax.broadcasted_iota(jnp.int32, (SLC_CHUNK, TQ), 1)
    blocks_per_chunk = SLC_CHUNK // SLC_LEN

    last_k0 = ks_ref.shape[2] - SLC_CHUNK

    def key_chunk(c):
        k0 = pl.multiple_of(jnp.minimum(c * SLC_CHUNK, last_k0), SLC_CHUNK)
        return ks_ref[0, 0, pl.ds(k0, SLC_CHUNK), :]

    def value_chunk(c):
        return vst_ref[0, 0, :, pl.ds(pl.multiple_of(c * SLC_CHUNK, SLC_CHUNK), SLC_CHUNK)]

    def softmax_step(s_ref, p_ref, a_ref, c):
        k0 = c * SLC_CHUNK
        selx = jnp.concatenate(
            [jnp.broadcast_to(sel_s[pl.ds(blocks_per_chunk * c + j, 1), :], (SLC_LEN, TQ)) for j in range(blocks_per_chunk)],
            axis=0)
        negm = jnp.where(selx > 0.5, jnp.where(k0 + keyr <= t0 + tokc, 0.0, NEG), NEG)
        s = s_ref[...] + jnp.concatenate([negm] * NSA_R, axis=1)
        m_old = m_s[...]
        m_new = jnp.maximum(m_old, jnp.max(s, axis=0, keepdims=True))
        alpha = jnp.exp(m_old - m_new)
        p = jnp.exp(s - m_new)
        l_s[...] = alpha * l_s[...] + jnp.sum(p, axis=0, keepdims=True)
        m_s[...] = m_new
        a_ref[...] = alpha
        p_ref[...] = p.astype(BF16)

    def accumulate(p_ref, a_ref, c):
        acc_s[...] = a_ref[...] * acc_s[...] + _dot(value_chunk(c), p_ref[...])

    sa_s[...] = _dot(key_chunk(0), qa)
    pb_s[...] = jnp.zeros_like(pb_s)
    ab_s[...] = jnp.ones_like(ab_s)

    def slc_pair(c, carry):
        sb_s[...] = _dot(key_chunk(2 * c + 1), qa)
        softmax_step(sa_s, pa_s, aa_s, 2 * c)
        accumulate(pb_s, ab_s, jnp.maximum(2 * c - 1, 0))
        sa_s[...] = _dot(key_chunk(2 * c + 2), qa)
        softmax_step(sb_s, pb_s, ab_s, 2 * c + 1)
        accumulate(pa_s, aa_s, 2 * c)
        return carry

    n_pairs = lax.shift_right_logical(t0 + TQ + 2 * SLC_CHUNK - 1, SLC_CHUNK.bit_length())
    lax.fori_loop(0, n_pairs, slc_pair, 0)
    accumulate(pb_s, ab_s, 2 * n_pairs - 1)
    o_s = acc_s[...] / l_s[...]

    wlen = WIN + TQ
    k0 = pl.multiple_of(jnp.maximum(t0 - WIN, 0), TQ)
    wkey = lax.broadcasted_iota(jnp.int32, (wlen, TQ), 0)
    dist = t0 + lax.broadcasted_iota(jnp.int32, (wlen, TQ), 1) - (k0 + wkey)
    negm = jnp.where(dist >= 0, jnp.where(dist < WIN, 0.0, NEG), NEG)
    s = _dot(kw_ref[0, 0, pl.ds(k0, wlen), :], qa) + jnp.concatenate([negm] * NSA_R, axis=1)
    p = jnp.exp(s - jnp.max(s, axis=0, keepdims=True))
    o_w = _dot(vwt_ref[0, 0, :, pl.ds(k0, wlen)], p.astype(BF16)) / jnp.sum(p, axis=0, keepdims=True)

    gts = gate_ref[...]
    outs = []
    for h in range(NSA_R):
        cols = hcols[h]
        o = (gts[3 * h:3 * h + 1, :] * o_cmp[:, cols] + gts[3 * h + 1:3 * h + 2, :] * o_s[:, cols]
             + gts[3 * h + 2:3 * h + 3, :] * o_w[:, cols])
        outs.append(o * lax.rsqrt(jnp.mean(o * o, axis=0, keepdims=True) + EPS))
    o_ref[...] = (jnp.concatenate(outs, axis=0).T * og_ref[...]).astype(o_ref.dtype)


def _nsa(qt, gates_t, kc, vct, ks, vst, kw, vwt, mconvt, og):
    B, _, nq, _, width = qt.shape
    T = nq * TQ
    ncp = kc.shape[2]
    rows = pl.BlockSpec((1, 1, T, 2 * NSA_DH), lambda b, g, i: (b, g, 0, 0))
    cols = pl.BlockSpec((1, 1, NSA_DH, T), lambda b, g, i: (b, g, 0, 0))
    ow = NSA_R * NSA_DH
    return pl.pallas_call(
        _nsa_kernel,
        grid=(B, NSA_G, nq),
        in_specs=[pl.BlockSpec((1, 1, 1, 2 * NSA_DH, width), lambda b, g, i: (b, g, i, 0, 0)),
                  pl.BlockSpec((128, TQ), lambda b, g, i: (g, b * nq + i)),
                  pl.BlockSpec((1, 1, ncp, NSA_DH), lambda b, g, i: (b, g, 0, 0)),
                  pl.BlockSpec((1, 1, NSA_DH, ncp), lambda b, g, i: (b, g, 0, 0)),
                  rows, cols, rows, cols,
                  pl.BlockSpec(mconvt.shape, lambda b, g, i: (0, 0)),
                  pl.BlockSpec((1, ow), lambda b, g, i: (0, g))],
        out_specs=pl.BlockSpec((TQ, ow), lambda b, g, i: (b * nq + i, g)),
        out_shape=jax.ShapeDtypeStruct((B * T, NSA_G * ow), BF16),
        scratch_shapes=[pltpu.VMEM((1, width), F32), pltpu.VMEM((1, width), F32),
                        pltpu.VMEM((NSA_DH, width), F32), pltpu.VMEM((128, TQ), F32),
                        pltpu.VMEM((SLC_CHUNK, width), F32), pltpu.VMEM((SLC_CHUNK, width), F32),
                        pltpu.VMEM((SLC_CHUNK, width), BF16), pltpu.VMEM((SLC_CHUNK, width), BF16),
                        pltpu.VMEM((1, width), F32), pltpu.VMEM((1, width), F32)],
        compiler_params=_params("parallel", "parallel", "arbitrary"),
        name="nsa_attn",
    )(qt, gates_t, kc, vct, ks, vst, kw, vwt, mconvt, og)


def _peer_sel_kernel(x_ref, ohg_ref, onsa_ref, wo1_ref, wo2_ref, fg_ref, wqt_ref, keys_ref,
                     h1_ref, xnt_ref, rank2_ref, e2_ref, lim1_ref, e1_ref, qt_s, r1_s, r2_s, lim_s, top_s):
    h1 = x_ref[...] + _dot(ohg_ref[...], wo1_ref[...]) + _dot(onsa_ref[...], wo2_ref[...])
    h1_ref[...] = h1
    xnt = (_rms(h1) * fg_ref[...]).T.astype(BF16)
    xnt_ref[...] = xnt
    qt_s[...] = _dot(wqt_ref[...], xnt)
    tb = xnt.shape[1]
    K = PEER_TOPK
    rowi = lax.broadcasted_iota(jnp.int32, (PEER_NKEYS, tb), 0).astype(F32)
    k1i = lax.broadcasted_iota(jnp.int32, (K, tb), 0).astype(F32)
    r = lax.broadcasted_iota(jnp.int32, (K + 7 * 8 + 8, tb), 0)
    ci = jnp.where(r < K, r, jnp.where(r < K + 56, K * (1 + jnp.right_shift(r - K, 3)) + jnp.bitwise_and(r - K, 7),
                                       K * (r - (K + 56) + 8))).astype(F32)

    mark = [REMOVED + k * MARK_STEP for k in range(K)]

    def topk_exact(s):
        rank = jnp.full(s.shape, float(PEER_NKEYS), F32)
        tops = []
        for k in range(K):
            m = jnp.max(s, axis=0, keepdims=True)
            idx = jnp.min(jnp.where(s == m, rowi, float(PEER_NKEYS)), axis=0, keepdims=True)
            pick = rowi == idx
            rank = jnp.where(pick, float(k), rank)
            s = jnp.where(pick, REMOVED, s)
            tops.append(m)
        return rank, jnp.concatenate(tops, axis=0)

    def topk_fast(s):
        tops = []
        for k in range(K):
            m = jnp.max(s, axis=0, keepdims=True)
            s = jnp.where(s == m, mark[k], s)
            tops.append(m)
        taken = s < MARK_LIMIT
        rank = jnp.where(taken, jnp.floor((s - mark[0]) * (1.0 / MARK_STEP) + 0.5), float(PEER_NKEYS))
        count = jnp.sum(jnp.where(taken, 1.0, 0.0), axis=0, keepdims=True)
        return rank, jnp.concatenate(tops, axis=0), count

    def candidates(t1, t2):
        return jnp.concatenate([t1[0:1, :] + t2] + [t1[k:k + 1, :] + t2[0:8, :] for k in range(1, 8)]
                               + [t1[8:K, :] + t2[0:1, :]], axis=0)

    def pairs_exact(t1, t2):
        cand = candidates(t1, t2)
        m0 = t1[0:1, :] + t2[0:1, :]
        lim = jnp.zeros((K, tb), F32)
        z = jnp.zeros((1, tb), F32)
        for k in range(K):
            m = jnp.max(cand, axis=0, keepdims=True)
            idx = jnp.min(jnp.where(cand == m, ci, float(K * K)), axis=0, keepdims=True)
            cand = jnp.where(ci == idx, REMOVED, cand)
            lim = lim + jnp.where(k1i == jnp.floor(idx * (1.0 / K)), 1.0, 0.0)
            z = z + jnp.exp(m - m0)
        return lim, z

    def pairs_fast(t1, t2):
        cand = candidates(t1, t2)
        m0 = t1[0:1, :] + t2[0:1, :]
        z = jnp.zeros((1, tb), F32)
        for k in range(K):
            m = jnp.max(cand, axis=0, keepdims=True)
            cand = jnp.where(cand == m, REMOVED, cand)
            z = z + jnp.exp(m - m0)
        taken = jnp.where(cand < MARK_LIMIT, 1.0, 0.0)
        rowsum = lambda a, b: jnp.sum(taken[a:b, :], axis=0, keepdims=True)
        lim = jnp.concatenate([rowsum(0, K)] + [rowsum(K + 8 * j, K + 8 * (j + 1)) for j in range(7)]
                              + [taken[K + 56:, :]], axis=0)
        return lim, z, jnp.sum(taken, axis=0, keepdims=True)

    def store_selection(r1, r2, t1, t2, lim, z):
        r1_s[...] = r1
        r2_s[...] = r2
        lim_s[...] = lim
        top_s[0:1, :] = t1[0:1, :]
        top_s[1:2, :] = t2[0:1, :]
        top_s[2:3, :] = z

    def head(h, carry):
        r = pl.multiple_of(h * 2 * PEER_NKEYS, 2 * PEER_NKEYS)
        s1 = _dot(keys_ref[2 * h], qt_s[pl.ds(r, PEER_NKEYS), :].astype(BF16))
        s2 = _dot(keys_ref[2 * h + 1], qt_s[pl.ds(r + PEER_NKEYS, PEER_NKEYS), :].astype(BF16))
        r1, t1, n1 = topk_fast(s1)
        r2, t2, n2 = topk_fast(s2)
        lim, z, n3 = pairs_fast(t1, t2)
        store_selection(r1, r2, t1, t2, lim, z)
        off_count = jnp.abs(n1 - K) + jnp.abs(n2 - K) + jnp.abs(n3 - K)

        @pl.when(jnp.max(off_count) > 0.5)
        def _():
            r1, t1 = topk_exact(s1)
            r2, t2 = topk_exact(s2)
            lim, z = pairs_exact(t1, t2)
            store_selection(r1, r2, t1, t2, lim, z)

        r1 = r1_s[...]
        lim = lim_s[...]
        lim1 = jnp.zeros((PEER_NKEYS, tb), F32)
        for k in range(K):
            lim1 = lim1 + jnp.where(r1 == float(k), lim[k:k + 1, :], 0.0)
        e1 = jnp.exp(s1 - top_s[0:1, :])
        e2 = jnp.exp(s2 - top_s[1:2, :]) / top_s[2:3, :]
        r2 = r2_s[...]
        for ts in range(tb // 128):
            cols = slice(128 * ts, 128 * (ts + 1))
            rank2_ref[h, ts] = pltpu.bitcast(r2[:, cols].astype(BF16), jnp.uint32)
            lim1_ref[h, ts] = lim1[:, cols]
            e1_ref[h, ts] = e1[:, cols]
            e2_ref[h, ts] = pltpu.bitcast(e2[:, cols].astype(BF16), jnp.uint32)
        return carry

    lax.fori_loop(0, PEER_HEADS, head, 0)


def _peer_select(x2, ohg, onsa, wo1, wo2, fg, wqt, keys, tb):
    N, D = x2.shape
    full = lambda a: pl.BlockSpec(a.shape, lambda i: (0,) * a.ndim)
    sel_spec = pl.BlockSpec((PEER_HEADS, tb // 128, PEER_NKEYS, 128), lambda i: (0, i, 0, 0))
    sel_shape = jax.ShapeDtypeStruct((PEER_HEADS, N // 128, PEER_NKEYS, 128), F32)
    packed_spec = pl.BlockSpec((PEER_HEADS, tb // 128, PEER_NKEYS // 2, 128), lambda i: (0, i, 0, 0))
    packed_shape = jax.ShapeDtypeStruct((PEER_HEADS, N // 128, PEER_NKEYS // 2, 128), jnp.uint32)
    return pl.pallas_call(
        _peer_sel_kernel,
        grid=(N // tb,),
        in_specs=[pl.BlockSpec((tb, D), lambda i: (i, 0)), pl.BlockSpec((tb, ohg.shape[1]), lambda i: (i, 0)),
                  pl.BlockSpec((tb, onsa.shape[1]), lambda i: (i, 0)),
                  full(wo1), full(wo2), full(fg), full(wqt), full(keys)],
        out_specs=[pl.BlockSpec((tb, D), lambda i: (i, 0)), pl.BlockSpec((D, tb), lambda i: (0, i)),
                   packed_spec, packed_spec, sel_spec, sel_spec],
        out_shape=[jax.ShapeDtypeStruct((N, D), F32), jax.ShapeDtypeStruct((D, N), BF16),
                   packed_shape, packed_shape, sel_shape, sel_shape],
        scratch_shapes=[pltpu.VMEM((wqt.shape[0], tb), F32), pltpu.VMEM((PEER_NKEYS, tb), F32),
                        pltpu.VMEM((PEER_NKEYS, tb), F32), pltpu.VMEM((PEER_TOPK, tb), F32), pltpu.VMEM((8, tb), F32)],
        compiler_params=_params("parallel"),
        name="peer_select",
    )(x2, ohg, onsa, wo1, wo2, fg, wqt, keys)


def _peer_dense_kernel(xnt_ref, u_hbm, vt_hbm, rank2_ref, e2_ref, lim1_ref, e1_ref, h1_ref, p_ref,
                       pproj_ref, pgn_ref, pgw_ref, out_ref,
                       acc_ref, at0_s, at1_s, wg0_s, wg1_s, u0_s, u1_s, v0_s, v1_s, xs_s, r2_s, e2_s, lim_s, e1_s,
                       u_sem, v_sem, *, eb, ne):
    e = pl.program_id(1)
    at_s, wg_s, u_s, v_s = (at0_s, at1_s), (wg0_s, wg1_s), (u0_s, u1_s), (v0_s, v1_s)

    def u_copy(block, slot):
        return pltpu.make_async_copy(u_hbm.at[pl.ds(pl.multiple_of(block * eb, eb), eb), :], u_s[slot], u_sem.at[slot])

    def v_copy(block, slot):
        return pltpu.make_async_copy(vt_hbm.at[block], v_s[slot], v_sem.at[slot])

    @pl.when(e == 0)
    def _():
        u_copy(0, 0).start()
        acc_ref[...] = jnp.zeros_like(acc_ref)
        for buf in (at0_s, at1_s, wg0_s, wg1_s, v0_s, v1_s):
            buf[...] = jnp.zeros_like(buf)
        xs_s[...] = xnt_ref[...]
        r2_s[...] = rank2_ref[...]
        e2_s[...] = e2_ref[...]
        lim_s[...] = lim1_ref[...]
        e1_s[...] = e1_ref[...]

    for par in range(2):
        @pl.when((e % 2 == par) & (e < ne))
        def _(par=par):
            u_copy(e, par).wait()

        @pl.when((e % 2 == par) & (e >= 2))
        def _(par=par):
            v_copy(e - 2, par).wait()

        @pl.when((e % 2 == par) & (e + 1 < ne))
        def _(par=par):
            u_copy(e + 1, 1 - par).start()

        @pl.when((e % 2 == par) & (e >= 1) & (e <= ne))
        def _(par=par):
            v_copy(e - 1, 1 - par).start()

    groups = eb // PEER_NKEYS
    half = PEER_NKEYS // 2
    tb = xs_s.shape[1]
    blk_b = jnp.clip(e - 1, 0, ne - 1)

    def weighting(par, ts):
        cols = slice(128 * ts, 128 * (ts + 1))
        for j0 in range(0, groups, 2):
            wt = [None, None]
            for h in range(PEER_HEADS):
                r2 = pltpu.bitcast(r2_s[h, half * ts:half * (ts + 1), :], BF16)
                w2 = pltpu.bitcast(e2_s[h, half * ts:half * (ts + 1), :], BF16)
                for dj in range(2):
                    i1 = blk_b * groups + j0 + dj
                    lim = jnp.broadcast_to(lim_s[h, ts, pl.ds(i1, 1), :].astype(BF16), (PEER_NKEYS, 128))
                    w1 = jnp.broadcast_to(e1_s[h, ts, pl.ds(i1, 1), :].astype(BF16), (PEER_NKEYS, 128))
                    w = jnp.where(r2 < lim, w2, 0.0) * w1
                    wt[dj] = w if wt[dj] is None else wt[dj] + w
            for dj in range(2):
                rows = slice(PEER_NKEYS * (j0 + dj), PEER_NKEYS * (j0 + dj + 1))
                wg_s[1 - par][rows, cols] = wt[dj] * jax.nn.gelu(at_s[1 - par][rows, cols]).astype(BF16)

    def stages(par):
        nq = tb // 128
        hw = tb // 2
        for q in range(nq):
            cols = slice(hw * (q % 2), hw * (q % 2 + 1))
            if q < nq // 2:
                at_s[par][:, cols] = _dot(u_s[par][...], xs_s[:, cols])
            else:
                acc_ref[:, cols] += _dot(v_s[par][...], wg_s[par][:, cols])
            weighting(par, q)

    for par in range(2):
        @pl.when(e % 2 == par)
        def _(par=par):
            stages(par)

    @pl.when(e == ne + 1)
    def _():
        h2 = h1_ref[...] + acc_ref[...].T
        gate = jax.nn.sigmoid(_dot((_rms(h2) * pgn_ref[...]).astype(BF16), pgw_ref[...]))
        out_ref[...] = h2 + _dot(p_ref[...].astype(BF16), pproj_ref[...]) * gate


def _peer_dense(xnt, u, vt, rank2, e2, lim1, e1, h1, p2, pproj, pgn, pgw, tb, eb):
    D, N = xnt.shape
    ne = u.shape[0] // eb
    full = lambda a: pl.BlockSpec(a.shape, lambda t, e: (0,) * a.ndim)
    sel_block = (PEER_HEADS, tb // 128, PEER_NKEYS, 128)
    sel_spec = pl.BlockSpec(sel_block, lambda t, e: (0, t, 0, 0))
    packed_block = (PEER_HEADS, tb // 128 * (PEER_NKEYS // 2), 128)
    packed_spec = pl.BlockSpec(packed_block, lambda t, e: (0, t, 0))
    rank2 = rank2.reshape(PEER_HEADS, -1, 128)
    e2 = e2.reshape(PEER_HEADS, -1, 128)
    any_spec = pl.BlockSpec(memory_space=pl.ANY)
    return pl.pallas_call(
        functools.partial(_peer_dense_kernel, eb=eb, ne=ne),
        grid=(N // tb, ne + 2),
        in_specs=[pl.BlockSpec((D, tb), lambda t, e: (0, t)), any_spec, any_spec,
                  packed_spec, packed_spec, sel_spec, sel_spec,
                  pl.BlockSpec((tb, D), lambda t, e: (t, 0)), pl.BlockSpec((tb, p2.shape[1]), lambda t, e: (t, 0)),
                  full(pproj), full(pgn), full(pgw)],
        out_specs=pl.BlockSpec((tb, D), lambda t, e: (t, 0)),
        out_shape=jax.ShapeDtypeStruct((N, D), F32),
        scratch_shapes=[pltpu.VMEM((D, tb), F32),
                        pltpu.VMEM((eb, tb), F32), pltpu.VMEM((eb, tb), F32),
                        pltpu.VMEM((eb, tb), BF16), pltpu.VMEM((eb, tb), BF16),
                        pltpu.VMEM((eb, D), BF16), pltpu.VMEM((eb, D), BF16),
                        pltpu.VMEM((D, eb), BF16), pltpu.VMEM((D, eb), BF16),
                        pltpu.VMEM((D, tb), BF16),
                        pltpu.VMEM(packed_block, jnp.uint32), pltpu.VMEM(packed_block, jnp.uint32),
                        pltpu.VMEM(sel_block, F32), pltpu.VMEM(sel_block, F32),
                        pltpu.SemaphoreType.DMA((2,)), pltpu.SemaphoreType.DMA((2,))],
        compiler_params=_params("arbitrary", "arbitrary"),
        name="peer_dense",
    )(xnt, u, vt, rank2, e2, lim1, e1, h1, p2, pproj, pgn, pgw)


def _conv_matrix(ncp):
    ratio = SLC_LEN // CMP_STRIDE
    weights = np.convolve(np.ones(ratio), np.ones(CMP_LEN // CMP_STRIDE))
    m = np.zeros((ncp, 128), np.float32)
    for j in range(128):
        for o, w in enumerate(weights):
            if ratio * j + o < ncp:
                m[ratio * j + o, j] = w
    return jnp.asarray(m)


def _block(n, pref):
    while n % pref:
        pref //= 2
    return pref


def kernel(x, p, mix_norm, w_in, hg_lb_logits, hg_out_norm, nsa_q_norm, nsa_k_norm, cmp_pe, cmp_w1, cmp_w2,
           nsa_out_norm, w_out, ffn_norm, peer_wq, peer_keys, peer_u, peer_v, ple_proj, ple_gate_norm, ple_gate_w):
    B, T, D = x.shape
    N = B * T
    x2 = x.reshape(N, D)
    nsa_w = NSA_HEADS * NSA_DH
    kvw = NSA_G * NSA_DH

    w = w_in[0]
    c0 = 4 * HG_WIDTH
    whg = w[:, :c0].astype(BF16)
    wq = w[:, c0:c0 + nsa_w].astype(BF16)
    wkv = w[:, c0 + nsa_w:c0 + nsa_w + 6 * kvw].astype(BF16)
    wg = w[:, c0 + nsa_w + 6 * kvw:]
    per_g = 3 * NSA_R
    wgate = jnp.zeros((D, 256), F32)
    for g in range(NSA_G):
        wgate = wgate.at[:, 128 * g:128 * g + per_g].set(wg[:, per_g * g:per_g * (g + 1)])
    wgate = wgate.astype(BF16)

    hg, qt, kcr, vcr, ks, vst, kw, vwt, gates_t = _inproj(
        x2, mix_norm, whg, wq, wkv, wgate, nsa_q_norm.reshape(NSA_DH, 1), nsa_k_norm[0], B, T, _block(T, 256))

    o_hg = _hgrn(hg.reshape(B, T, 4 * HG_WIDTH), hg_lb_logits, hg_out_norm, _block(T // HG_CHUNK, 4))

    ncp = T // CMP_STRIDE
    stride_rows = lambda a: a.reshape(B * NSA_G, ncp, CMP_STRIDE * NSA_DH)
    pe8 = jnp.broadcast_to(cmp_pe[0].reshape(2, 1, CMP_LEN * NSA_DH), (2, 8, CMP_LEN * NSA_DH))
    kc, vct = _compress(stride_rows(kcr), stride_rows(vcr), cmp_w1[0], cmp_w2[0], pe8, nsa_k_norm[0])
    kc = kc.reshape(B, NSA_G, ncp, NSA_DH)
    vct = vct.reshape(B, NSA_G, NSA_DH, ncp)

    o_nsa = _nsa(qt, gates_t, kc, vct, ks, vst, kw, vwt, _conv_matrix(ncp).T, nsa_out_norm)

    wo = w_out[0].astype(BF16)
    wqt = peer_wq[0].T.astype(BF16)
    keys = peer_keys[0].reshape(2 * PEER_HEADS, PEER_NKEYS, -1).astype(BF16)
    h1, xnt, rank2, e2, lim1, e1 = _peer_select(
        x2, o_hg.reshape(N, HG_WIDTH), o_nsa, wo[:HG_WIDTH], wo[HG_WIDTH:], ffn_norm, wqt, keys, _block(N, 256))

    eb = 512
    vt = peer_v[0].reshape(-1, eb, D).transpose(0, 2, 1).astype(BF16)
    out = _peer_dense(xnt, peer_u[0].astype(BF16), vt, rank2, e2, lim1, e1, h1,
                      p[0].reshape(N, -1), ple_proj[0].astype(BF16), ple_gate_norm, ple_gate_w[0].astype(BF16),
                      _block(N, 512), eb)
    return out.reshape(B, T, D)
```

```python
import functools

import numpy as np
import jax
import jax.numpy as jnp
from jax import lax
from jax.experimental import pallas as pl
from jax.experimental.pallas import tpu as pltpu

F32 = jnp.float32
BF16 = jnp.bfloat16
HIGHEST = lax.Precision.HIGHEST
EPS = 1e-6
NEG = -1e30
REMOVED = -3e38
MARK_STEP = 2e36
MARK_LIMIT = -2.5e38

HG_HEADS = 4
HG_D = 128
HG_CHUNK = 64
HG_WIDTH = HG_HEADS * HG_D
NSA_DH = 64
NSA_HEADS = 8
NSA_G = 2
NSA_R = NSA_HEADS // NSA_G
CMP_LEN = 32
CMP_STRIDE = 16
SLC_LEN = 64
SLC_TOPK = 16
WIN = 512
TQ = 128
SLC_CHUNK = 256
PEER_HEADS = 8
PEER_NKEYS = 128
PEER_TOPK = 16
VMEM_LIMIT = 56 * 1024 * 1024


def _dot(a, b, precision=None):
    return lax.dot_general(a, b, (((1,), (0,)), ((), ())), preferred_element_type=F32, precision=precision)


def _dot_nt(a, b):
    return lax.dot_general(a, b, (((1,), (1,)), ((), ())), preferred_element_type=F32)


def _dot_tn(a, b):
    return lax.dot_general(a, b, (((0,), (0,)), ((), ())), preferred_element_type=F32)


def _rms(x):
    return x * lax.rsqrt(jnp.mean(x * x, axis=-1, keepdims=True) + EPS)


def _params(*sem):
    return pltpu.CompilerParams(dimension_semantics=sem, vmem_limit_bytes=VMEM_LIMIT)


def _alibi_slope_row(g, width):
    lane_h = jnp.right_shift(lax.broadcasted_iota(jnp.int32, (1, width), 1), TQ.bit_length() - 1)
    row = jnp.zeros((1, width), F32)
    for h in range(NSA_R):
        row = jnp.where(lane_h == h, jnp.where(g == 0, 2.0 ** -(h + 1), 2.0 ** -(NSA_R + h + 1)).astype(F32), row)
    return row


def _inproj_kernel(x_ref, g_ref, whg_ref, wq_ref, wkv_ref, wgate_ref, qgc_ref, kg_ref,
                   hg_ref, qt_ref, kcr_ref, vcr_ref, ks_ref, vst_ref, kw_ref, vwt_ref, gate_ref, *, nt):
    a = (_rms(x_ref[...]) * g_ref[...]).astype(BF16)
    hg_ref[...] = _dot(a, whg_ref[...])
    tb = a.shape[0]
    nqb = tb // TQ
    width = NSA_R * TQ

    qt = _dot(a, wq_ref[...]).T
    aug_row = lax.broadcasted_iota(jnp.int32, (NSA_DH, width), 0)
    for g in range(NSA_G):
        slope = _alibi_slope_row(g, width)
        aug = jnp.where(aug_row == 0, slope, jnp.where(aug_row == 1, slope * TQ, 0.0)).astype(BF16)
        for j in range(nqb):
            qt_ref[0, g, j, NSA_DH:2 * NSA_DH, :] = aug
        for h in range(NSA_R):
            blk = qt[NSA_DH * (NSA_R * g + h):NSA_DH * (NSA_R * g + h + 1)]
            r = lax.rsqrt(jnp.mean(blk * blk, axis=0, keepdims=True) + EPS)
            qn = (blk * r * qgc_ref[...] * NSA_DH ** -0.5).astype(BF16)
            for j in range(nqb):
                qt_ref[0, g, j, 0:NSA_DH, TQ * h:TQ * (h + 1)] = qn[:, TQ * j:TQ * (j + 1)]

    kv = _dot(a, wkv_ref[...])
    kvt = kv.T
    pos = (pl.program_id(0) % nt) * tb + lax.broadcasted_iota(jnp.int32, (tb, NSA_DH), 0)
    aug_col = lax.broadcasted_iota(jnp.int32, (tb, NSA_DH), 1)
    pos_aug = jnp.where(aug_col == 0, jnp.bitwise_and(pos, TQ - 1),
                        jnp.where(aug_col == 1, jnp.right_shift(pos, TQ.bit_length() - 1), 0)).astype(F32)

    def part(j, g):
        return kv[:, 128 * j + NSA_DH * g:128 * j + NSA_DH * (g + 1)]

    def part_t(j, g):
        return kvt[128 * j + NSA_DH * g:128 * j + NSA_DH * (g + 1)]

    for g in range(NSA_G):
        kcr_ref[0, g] = part(0, g)
        vcr_ref[0, g] = part(1, g)
        ks_ref[0, g] = jnp.concatenate([_rms(part(2, g)) * kg_ref[1:2, :], pos_aug], axis=1).astype(BF16)
        vst_ref[0, g] = part_t(3, g).astype(BF16)
        kw_ref[0, g] = jnp.concatenate([_rms(part(4, g)) * kg_ref[2:3, :], pos_aug], axis=1).astype(BF16)
        vwt_ref[0, g] = part_t(5, g).astype(BF16)
    gate_ref[...] = jax.nn.sigmoid(_dot(a, wgate_ref[...])).T


def _inproj(x2, g, whg, wq, wkv, wgate, qgc, kg, B, T, tb):
    N, D = x2.shape
    nt = T // tb
    nqb = tb // TQ
    full = lambda shape: pl.BlockSpec(shape, lambda i: (0,) * len(shape))
    row_spec = lambda w: pl.BlockSpec((1, NSA_G, tb, w), lambda i: (i // nt, 0, i % nt, 0))
    col_spec = pl.BlockSpec((1, NSA_G, NSA_DH, tb), lambda i: (i // nt, 0, 0, i % nt))
    row_shape = lambda w, dt: jax.ShapeDtypeStruct((B, NSA_G, T, w), dt)
    col_shape = jax.ShapeDtypeStruct((B, NSA_G, NSA_DH, T), BF16)
    return pl.pallas_call(
        functools.partial(_inproj_kernel, nt=nt),
        grid=(N // tb,),
        in_specs=[pl.BlockSpec((tb, D), lambda i: (i, 0)), full(g.shape), full(whg.shape), full(wq.shape),
                  full(wkv.shape), full(wgate.shape), full(qgc.shape), full(kg.shape)],
        out_specs=[pl.BlockSpec((tb, 4 * HG_WIDTH), lambda i: (i, 0)),
                   pl.BlockSpec((1, NSA_G, nqb, 2 * NSA_DH, NSA_R * TQ), lambda i: (i // nt, 0, i % nt, 0, 0)),
                   row_spec(NSA_DH), row_spec(NSA_DH), row_spec(2 * NSA_DH), col_spec, row_spec(2 * NSA_DH), col_spec,
                   pl.BlockSpec((256, tb), lambda i: (0, i))],
        out_shape=[jax.ShapeDtypeStruct((N, 4 * HG_WIDTH), F32),
                   jax.ShapeDtypeStruct((B, NSA_G, T // TQ, 2 * NSA_DH, NSA_R * TQ), BF16),
                   row_shape(NSA_DH, F32), row_shape(NSA_DH, F32), row_shape(2 * NSA_DH, BF16), col_shape,
                   row_shape(2 * NSA_DH, BF16), col_shape,
                   jax.ShapeDtypeStruct((256, N), F32)],
        compiler_params=_params("parallel"),
        name="inproj",
    )(x2, g, whg, wq, wkv, wgate, qgc, kg)


def _hgrn_kernel(hg_ref, lbl_ref, gain_ref, o_ref, st_ref, *, nb, cpb):
    @pl.when(pl.program_id(0) == 0)
    def _():
        st_ref[...] = jnp.zeros_like(st_ref)

    logits = lbl_ref[...]
    ex = jnp.exp(logits - jnp.max(logits, axis=0, keepdims=True))
    lb_all = ex[0:1, :] / jnp.sum(ex, axis=0, keepdims=True)
    C = HG_CHUNK
    row = lax.broadcasted_iota(jnp.int32, (C, C), 0)
    col = lax.broadcasted_iota(jnp.int32, (C, C), 1)
    tril = row >= col
    crow = lax.broadcasted_iota(jnp.int32, (C, HG_D), 0)

    def chunk(c, carry):
        r0 = pl.multiple_of(c * C, C)
        rows = pl.ds(r0, C)
        for b in range(nb):
            for h in range(HG_HEADS):
                sl = slice(HG_D * h, HG_D * (h + 1))
                q, f, iv, g = [hg_ref[b, rows, j * HG_WIDTH + HG_D * h:j * HG_WIDTH + HG_D * (h + 1)] for j in range(4)]
                st = st_ref[b * HG_HEADS + h]
                lb = lb_all[:, sl]
                forget = lb + (1.0 - lb) * jax.nn.sigmoid(f)
                logf = jnp.log(forget)
                key = 1.0 - forget
                query = jax.nn.silu(q) * HG_D ** -0.5
                G = logf
                for d in (1, 2, 4, 8, 16, 32):
                    G = G + jnp.where(crow >= d, pltpu.roll(G, d, 0), 0.0)
                g_ref = G[C // 2 - 1:C // 2, :]
                g_last = G[C - 1:C, :]
                qa = (query * jnp.exp(G - g_ref)).astype(BF16)
                ka = (key * jnp.exp(g_ref - G)).astype(BF16)
                a = jnp.where(tril, _dot_nt(qa, ka), 0.0)
                vb = iv.astype(BF16)
                o = _dot(a.astype(BF16), vb)
                o = o + _dot_nt((query * jnp.exp(G)).astype(BF16), st.astype(BF16))
                kb = (key * jnp.exp(g_last - G)).astype(BF16)
                st_ref[b * HG_HEADS + h] = st * jnp.exp(g_last) + _dot_tn(vb, kb)
                on = _rms(o) * gain_ref[:, sl] * jax.nn.silu(g)
                o_ref[b, rows, sl] = on.astype(o_ref.dtype)
        return carry

    lax.fori_loop(0, cpb, chunk, 0)


def _hgrn(hg3, lb_logits, gain, cpb):
    B, T, W = hg3.shape
    rows = cpb * HG_CHUNK
    return pl.pallas_call(
        functools.partial(_hgrn_kernel, nb=B, cpb=cpb),
        grid=(T // rows,),
        in_specs=[pl.BlockSpec((B, rows, W), lambda i: (0, i, 0)),
                  pl.BlockSpec(lb_logits.shape, lambda i: (0, 0)),
                  pl.BlockSpec(gain.shape, lambda i: (0, 0))],
        out_specs=pl.BlockSpec((B, rows, HG_WIDTH), lambda i: (0, i, 0)),
        out_shape=jax.ShapeDtypeStruct((B, T, HG_WIDTH), BF16),
        scratch_shapes=[pltpu.VMEM((B * HG_HEADS, HG_D, HG_D), F32)],
        compiler_params=_params("arbitrary"),
        name="hgrn2",
    )(hg3, lb_logits, gain)


def _cmp_kernel(xk_ref, xv_ref, w1_ref, w2_ref, pe_ref, kg_ref, kc_ref, vct_ref):
    half = CMP_STRIDE * NSA_DH
    ys = []
    for idx, x_ref in enumerate((xk_ref, xv_ref)):
        x = x_ref[0]
        n = x.shape[0]
        w1 = w1_ref[idx]
        first = _dot(x, w1[:half], HIGHEST)
        second = _dot(x, w1[half:], HIGHEST)
        const = _dot(pe_ref[idx], w1, HIGHEST)[0:1, :]
        y = jax.nn.gelu(first + pltpu.roll(second, n - 1, 0) + const)
        ys.append(_dot(y, w2_ref[idx], HIGHEST))
    kc_ref[0] = (_rms(ys[0]) * kg_ref[0:1, :]).astype(BF16)
    vct_ref[0] = jnp.concatenate(ys, axis=1).T[NSA_DH:].astype(BF16)


def _compress(xk, xv, w1, w2, pe8, kg):
    BG, n, K = xk.shape
    full = lambda a: pl.BlockSpec(a.shape, lambda i: (0,) * a.ndim)
    blk = pl.BlockSpec((1, n, K), lambda i: (i, 0, 0))
    return pl.pallas_call(
        _cmp_kernel,
        grid=(BG,),
        in_specs=[blk, blk, full(w1), full(w2), full(pe8), full(kg)],
        out_specs=[pl.BlockSpec((1, n, NSA_DH), lambda i: (i, 0, 0)), pl.BlockSpec((1, NSA_DH, n), lambda i: (i, 0, 0))],
        out_shape=[jax.ShapeDtypeStruct((BG, n, NSA_DH), BF16), jax.ShapeDtypeStruct((BG, NSA_DH, n), BF16)],
        compiler_params=_params("parallel"),
        name="nsa_compress",
    )(xk, xv, w1, w2, pe8, kg)


def _nsa_kernel(qt_ref, gate_ref, kc_ref, vct_ref, ks_ref, vst_ref, kw_ref, vwt_ref, og_ref,
                o_ref, m_s, l_s, acc_s, sel_s, sa_s, sb_s, pa_s, pb_s, aa_s, ab_s, psum_s):
    g = pl.program_id(1)
    i = pl.program_id(2)
    t0 = i * TQ
    width = NSA_R * TQ
    qa = qt_ref[0, 0, 0]
    q = qa[0:NSA_DH]
    slope_row = _alibi_slope_row(g, width)
    hcols = [slice(h * TQ, (h + 1) * TQ) for h in range(NSA_R)]

    ncp = kc_ref.shape[2]
    nn = lax.broadcasted_iota(jnp.int32, (ncp, TQ), 0)
    tt = lax.broadcasted_iota(jnp.int32, (ncp, TQ), 1)
    dist_c = t0 + tt - (nn * CMP_STRIDE + CMP_LEN - 1)
    valid_c = dist_c >= 0
    dist_cf = dist_c.astype(F32)
    s_c = _dot(kc_ref[0, 0], q)
    psum = jnp.zeros((ncp, TQ), F32)
    probs = []
    for h in range(NSA_R):
        s = jnp.where(valid_c, s_c[:, hcols[h]] - slope_row[:, hcols[h]] * dist_cf, NEG)
        m = jnp.max(s, axis=0, keepdims=True)
        m = jnp.where(m > 0.1 * NEG, m, 0.0)
        e = jnp.where(valid_c, jnp.exp(s - m), 0.0)
        p = e / jnp.maximum(jnp.sum(e, axis=0, keepdims=True), 1e-30)
        psum = psum + p
        probs.append(p.astype(BF16))
    o_cmp = _dot(vct_ref[0, 0], jnp.concatenate(probs, axis=1))

    ratio = SLC_LEN // CMP_STRIDE
    weights = np.convolve(np.ones(ratio), np.ones(CMP_LEN // CMP_STRIDE))
    nsb = ncp // ratio
    psum_s[0:ncp, :] = psum
    psum_s[ncp:, :] = jnp.zeros((psum_s.shape[0] - ncp, TQ), F32)
    imp = float(weights[0]) * psum_s[pl.ds(0, nsb, stride=ratio), :]
    for o in range(1, len(weights)):
        imp = imp + float(weights[o]) * psum_s[pl.ds(o, nsb, stride=ratio), :]
    if nsb < 128:
        imp = jnp.concatenate([imp, jnp.zeros((128 - nsb, TQ), F32)], axis=0)
    blk = lax.broadcasted_iota(jnp.int32, (128, TQ), 0).astype(F32)
    tcol = lax.broadcasted_iota(jnp.int32, (128, TQ), 1)
    cur = jnp.right_shift(t0 + tcol, SLC_LEN.bit_length() - 1).astype(F32)
    forced = (blk == 0.0) | (blk == cur) | (blk == cur - 1.0)
    v = jnp.where(forced, -NEG, jnp.where(blk <= cur, imp, NEG))
    sel = jnp.zeros((128, TQ), F32)
    for _ in range(SLC_TOPK):
        m = jnp.max(v, axis=0, keepdims=True)
        idx = jnp.min(jnp.where(v == m, blk, 128.0), axis=0, keepdims=True)
        pick = blk == idx
        sel = jnp.where(pick, 1.0, sel)
        v = jnp.where(pick, REMOVED, v)
    sel_s[...] = sel

    m_s[...] = jnp.full(m_s.shape, NEG, F32)
    l_s[...] = jnp.zeros(l_s.shape, F32)
    acc_s[...] = jnp.zeros(acc_s.shape, F32)
    keyr = lax.broadcasted_iota(jnp.int32, (SLC_CHUNK, TQ), 0)
    tokc = lax.broadcasted_iota(jnp.int32, (SLC_CHUNK, TQ), 1)
    blocks_per_chunk = SLC_CHUNK // SLC_LEN

    last_k0 = ks_ref.shape[2] - SLC_CHUNK

    def key_chunk(c):
        k0 = pl.multiple_of(jnp.minimum(c * SLC_CHUNK, last_k0), SLC_CHUNK)
        return ks_ref[0, 0, pl.ds(k0, SLC_CHUNK), :]

    def value_chunk(c):
        return vst_ref[0, 0, :, pl.ds(pl.multiple_of(c * SLC_CHUNK, SLC_CHUNK), SLC_CHUNK)]

    def softmax_step(s_ref, p_ref, a_ref, c):
        k0 = c * SLC_CHUNK
        selx = jnp.concatenate(
            [jnp.broadcast_to(sel_s[pl.ds(blocks_per_chunk * c + j, 1), :], (SLC_LEN, TQ)) for j in range(blocks_per_chunk)],
            axis=0)
        negm = jnp.where(selx > 0.5, jnp.where(k0 + keyr <= t0 + tokc, 0.0, NEG), NEG)
        s = s_ref[...] + jnp.concatenate([negm] * NSA_R, axis=1)
        m_old = m_s[...]
        m_new = jnp.maximum(m_old, jnp.max(s, axis=0, keepdims=True))
        alpha = jnp.exp(m_old - m_new)
        p = jnp.exp(s - m_new)
        l_s[...] = alpha * l_s[...] + jnp.sum(p, axis=0, keepdims=True)
        m_s[...] = m_new
        a_ref[...] = alpha
        p_ref[...] = p.astype(BF16)

    def accumulate(p_ref, a_ref, c):
        acc_s[...] = a_ref[...] * acc_s[...] + _dot(value_chunk(c), p_ref[...])

    sa_s[...] = _dot(key_chunk(0), qa)
    pb_s[...] = jnp.zeros_like(pb_s)
    ab_s[...] = jnp.ones_like(ab_s)

    def slc_pair(c, carry):
        sb_s[...] = _dot(key_chunk(2 * c + 1), qa)
        softmax_step(sa_s, pa_s, aa_s, 2 * c)
        accumulate(pb_s, ab_s, jnp.maximum(2 * c - 1, 0))
        sa_s[...] = _dot(key_chunk(2 * c + 2), qa)
        softmax_step(sb_s, pb_s, ab_s, 2 * c + 1)
        accumulate(pa_s, aa_s, 2 * c)
        return carry

    n_pairs = lax.shift_right_logical(t0 + TQ + 2 * SLC_CHUNK - 1, SLC_CHUNK.bit_length())
    lax.fori_loop(0, n_pairs, slc_pair, 0)
    accumulate(pb_s, ab_s, 2 * n_pairs - 1)
    o_s = acc_s[...] / l_s[...]

    wlen = WIN + TQ
    k0 = pl.multiple_of(jnp.maximum(t0 - WIN, 0), TQ)
    wkey = lax.broadcasted_iota(jnp.int32, (wlen, TQ), 0)
    dist = t0 + lax.broadcasted_iota(jnp.int32, (wlen, TQ), 1) - (k0 + wkey)
    negm = jnp.where(dist >= 0, jnp.where(dist < WIN, 0.0, NEG), NEG)
    s = _dot(kw_ref[0, 0, pl.ds(k0, wlen), :], qa) + jnp.concatenate([negm] * NSA_R, axis=1)
    p = jnp.exp(s - jnp.max(s, axis=0, keepdims=True))
    o_w = _dot(vwt_ref[0, 0, :, pl.ds(k0, wlen)], p.astype(BF16)) / jnp.sum(p, axis=0, keepdims=True)

    gts = gate_ref[...]
    outs = []
    for h in range(NSA_R):
        cols = hcols[h]
        o = (gts[3 * h:3 * h + 1, :] * o_cmp[:, cols] + gts[3 * h + 1:3 * h + 2, :] * o_s[:, cols]
             + gts[3 * h + 2:3 * h + 3, :] * o_w[:, cols])
        outs.append(o * lax.rsqrt(jnp.mean(o * o, axis=0, keepdims=True) + EPS))
    o_ref[...] = (jnp.concatenate(outs, axis=0).T * og_ref[...]).astype(o_ref.dtype)


def _nsa(qt, gates_t, kc, vct, ks, vst, kw, vwt, og):
    B, _, nq, _, width = qt.shape
    T = nq * TQ
    ncp = kc.shape[2]
    rows = pl.BlockSpec((1, 1, T, 2 * NSA_DH), lambda b, g, i: (b, g, 0, 0))
    cols = pl.BlockSpec((1, 1, NSA_DH, T), lambda b, g, i: (b, g, 0, 0))
    ow = NSA_R * NSA_DH
    return pl.pallas_call(
        _nsa_kernel,
        grid=(B, NSA_G, nq),
        in_specs=[pl.BlockSpec((1, 1, 1, 2 * NSA_DH, width), lambda b, g, i: (b, g, i, 0, 0)),
                  pl.BlockSpec((128, TQ), lambda b, g, i: (g, b * nq + i)),
                  pl.BlockSpec((1, 1, ncp, NSA_DH), lambda b, g, i: (b, g, 0, 0)),
                  pl.BlockSpec((1, 1, NSA_DH, ncp), lambda b, g, i: (b, g, 0, 0)),
                  rows, cols, rows, cols,
                  pl.BlockSpec((1, ow), lambda b, g, i: (0, g))],
        out_specs=pl.BlockSpec((TQ, ow), lambda b, g, i: (b * nq + i, g)),
        out_shape=jax.ShapeDtypeStruct((B * T, NSA_G * ow), BF16),
        scratch_shapes=[pltpu.VMEM((1, width), F32), pltpu.VMEM((1, width), F32),
                        pltpu.VMEM((NSA_DH, width), F32), pltpu.VMEM((128, TQ), F32),
                        pltpu.VMEM((SLC_CHUNK, width), F32), pltpu.VMEM((SLC_CHUNK, width), F32),
                        pltpu.VMEM((SLC_CHUNK, width), BF16), pltpu.VMEM((SLC_CHUNK, width), BF16),
                        pltpu.VMEM((1, width), F32), pltpu.VMEM((1, width), F32),
                        pltpu.VMEM((ncp + 8, TQ), F32)],
        compiler_params=_params("parallel", "parallel", "arbitrary"),
        name="nsa_attn",
    )(qt, gates_t, kc, vct, ks, vst, kw, vwt, og)


def _peer_sel_kernel(x_ref, ohg_ref, onsa_ref, wo1_ref, wo2_ref, fg_ref, wqt_ref, keys_ref,
                     h1_ref, xnt_ref, rank2_ref, e2_ref, lim1_ref, e1_ref, qt_s, r1_s, r2_s, lim_s, top_s):
    h1 = x_ref[...] + _dot(ohg_ref[...], wo1_ref[...]) + _dot(onsa_ref[...], wo2_ref[...])
    h1_ref[...] = h1
    xnt = (_rms(h1) * fg_ref[...]).T.astype(BF16)
    xnt_ref[...] = xnt
    qt_s[...] = _dot(wqt_ref[...], xnt)
    tb = xnt.shape[1]
    K = PEER_TOPK
    rowi = lax.broadcasted_iota(jnp.int32, (PEER_NKEYS, tb), 0).astype(F32)
    k1i = lax.broadcasted_iota(jnp.int32, (K, tb), 0).astype(F32)
    r = lax.broadcasted_iota(jnp.int32, (K + 7 * 8 + 8, tb), 0)
    ci = jnp.where(r < K, r, jnp.where(r < K + 56, K * (1 + jnp.right_shift(r - K, 3)) + jnp.bitwise_and(r - K, 7),
                                       K * (r - (K + 56) + 8))).astype(F32)

    mark = [REMOVED + k * MARK_STEP for k in range(K)]

    def topk_exact(s):
        rank = jnp.full(s.shape, float(PEER_NKEYS), F32)
        tops = []
        for k in range(K):
            m = jnp.max(s, axis=0, keepdims=True)
            idx = jnp.min(jnp.where(s == m, rowi, float(PEER_NKEYS)), axis=0, keepdims=True)
            pick = rowi == idx
            rank = jnp.where(pick, float(k), rank)
            s = jnp.where(pick, REMOVED, s)
            tops.append(m)
        return rank, jnp.concatenate(tops, axis=0)

    def topk_fast(s):
        tops = []
        for k in range(K):
            m = jnp.max(s, axis=0, keepdims=True)
            s = jnp.where(s == m, mark[k], s)
            tops.append(m)
        taken = s < MARK_LIMIT
        rank = jnp.where(taken, jnp.floor((s - mark[0]) * (1.0 / MARK_STEP) + 0.5), float(PEER_NKEYS))
        count = jnp.sum(jnp.where(taken, 1.0, 0.0), axis=0, keepdims=True)
        return rank, jnp.concatenate(tops, axis=0), count

    def candidates(t1, t2):
        return jnp.concatenate([t1[0:1, :] + t2] + [t1[k:k + 1, :] + t2[0:8, :] for k in range(1, 8)]
                               + [t1[8:K, :] + t2[0:1, :]], axis=0)

    def pairs_exact(t1, t2):
        cand = candidates(t1, t2)
        m0 = t1[0:1, :] + t2[0:1, :]
        lim = jnp.zeros((K, tb), F32)
        z = jnp.zeros((1, tb), F32)
        for k in range(K):
            m = jnp.max(cand, axis=0, keepdims=True)
            idx = jnp.min(jnp.where(cand == m, ci, float(K * K)), axis=0, keepdims=True)
            cand = jnp.where(ci == idx, REMOVED, cand)
            lim = lim + jnp.where(k1i == jnp.floor(idx * (1.0 / K)), 1.0, 0.0)
            z = z + jnp.exp(m - m0)
        return lim, z

    def pairs_fast(t1, t2):
        cand = candidates(t1, t2)
        m0 = t1[0:1, :] + t2[0:1, :]
        z = jnp.zeros((1, tb), F32)
        for k in range(K):
            m = jnp.max(cand, axis=0, keepdims=True)
            cand = jnp.where(cand == m, REMOVED, cand)
            z = z + jnp.exp(m - m0)
        taken = jnp.where(cand < MARK_LIMIT, 1.0, 0.0)
        rowsum = lambda a, b: jnp.sum(taken[a:b, :], axis=0, keepdims=True)
        lim = jnp.concatenate([rowsum(0, K)] + [rowsum(K + 8 * j, K + 8 * (j + 1)) for j in range(7)]
                              + [taken[K + 56:, :]], axis=0)
        return lim, z, jnp.sum(taken, axis=0, keepdims=True)

    def store_selection(r1, r2, t1, t2, lim, z):
        r1_s[...] = r1
        r2_s[...] = r2
        lim_s[...] = lim
        top_s[0:1, :] = t1[0:1, :]
        top_s[1:2, :] = t2[0:1, :]
        top_s[2:3, :] = z

    def head(h, carry):
        r = pl.multiple_of(h * 2 * PEER_NKEYS, 2 * PEER_NKEYS)
        s1 = _dot(keys_ref[2 * h], qt_s[pl.ds(r, PEER_NKEYS), :].astype(BF16))
        s2 = _dot(keys_ref[2 * h + 1], qt_s[pl.ds(r + PEER_NKEYS, PEER_NKEYS), :].astype(BF16))
        r1, t1, n1 = topk_fast(s1)
        r2, t2, n2 = topk_fast(s2)
        lim, z, n3 = pairs_fast(t1, t2)
        store_selection(r1, r2, t1, t2, lim, z)
        off_count = jnp.abs(n1 - K) + jnp.abs(n2 - K) + jnp.abs(n3 - K)

        @pl.when(jnp.max(off_count) > 0.5)
        def _():
            r1, t1 = topk_exact(s1)
            r2, t2 = topk_exact(s2)
            lim, z = pairs_exact(t1, t2)
            store_selection(r1, r2, t1, t2, lim, z)

        r1 = r1_s[...]
        lim = lim_s[...]
        rank_bits = r1.astype(jnp.int32)
        level = [lim[k:k + 1, :] for k in range(K)]
        for bit in range(K.bit_length() - 1):
            odd = jnp.bitwise_and(rank_bits, 1 << bit) != 0
            level = [jnp.where(odd, level[2 * i + 1], level[2 * i]) for i in range(len(level) // 2)]
        lim1 = jnp.where(r1 < float(K), level[0], 0.0)
        e1 = jnp.exp(s1 - top_s[0:1, :])
        e2 = jnp.exp(s2 - top_s[1:2, :]) / top_s[2:3, :]
        r2 = r2_s[...]
        for ts in range(tb // 128):
            cols = slice(128 * ts, 128 * (ts + 1))
            rank2_ref[h, ts] = pltpu.bitcast(r2[:, cols].astype(BF16), jnp.uint32)
            lim1_ref[h, ts] = lim1[:, cols]
            e1_ref[h, ts] = e1[:, cols]
            e2_ref[h, ts] = pltpu.bitcast(e2[:, cols].astype(BF16), jnp.uint32)
        return carry

    lax.fori_loop(0, PEER_HEADS, head, 0)


def _peer_select(x2, ohg, onsa, wo1, wo2, fg, wqt, keys, tb):
    N, D = x2.shape
    full = lambda a: pl.BlockSpec(a.shape, lambda i: (0,) * a.ndim)
    sel_spec = pl.BlockSpec((PEER_HEADS, tb // 128, PEER_NKEYS, 128), lambda i: (0, i, 0, 0))
    sel_shape = jax.ShapeDtypeStruct((PEER_HEADS, N // 128, PEER_NKEYS, 128), F32)
    packed_spec = pl.BlockSpec((PEER_HEADS, tb // 128, PEER_NKEYS // 2, 128), lambda i: (0, i, 0, 0))
    packed_shape = jax.ShapeDtypeStruct((PEER_HEADS, N // 128, PEER_NKEYS // 2, 128), jnp.uint32)
    return pl.pallas_call(
        _peer_sel_kernel,
        grid=(N // tb,),
        in_specs=[pl.BlockSpec((tb, D), lambda i: (i, 0)), pl.BlockSpec((tb, ohg.shape[1]), lambda i: (i, 0)),
                  pl.BlockSpec((tb, onsa.shape[1]), lambda i: (i, 0)),
                  full(wo1), full(wo2), full(fg), full(wqt), full(keys)],
        out_specs=[pl.BlockSpec((tb, D), lambda i: (i, 0)), pl.BlockSpec((D, tb), lambda i: (0, i)),
                   packed_spec, packed_spec, sel_spec, sel_spec],
        out_shape=[jax.ShapeDtypeStruct((N, D), F32), jax.ShapeDtypeStruct((D, N), BF16),
                   packed_shape, packed_shape, sel_shape, sel_shape],
        scratch_shapes=[pltpu.VMEM((wqt.shape[0], tb), F32), pltpu.VMEM((PEER_NKEYS, tb), F32),
                        pltpu.VMEM((PEER_NKEYS, tb), F32), pltpu.VMEM((PEER_TOPK, tb), F32), pltpu.VMEM((8, tb), F32)],
        compiler_params=_params("parallel"),
        name="peer_select",
    )(x2, ohg, onsa, wo1, wo2, fg, wqt, keys)


def _peer_dense_kernel(xnt_ref, u_hbm, vt_hbm, rank2_ref, e2_ref, lim1_ref, e1_ref, h1_ref, p_ref,
                       pproj_ref, pgn_ref, pgw_ref, out_ref,
                       acc_ref, at0_s, at1_s, wg0_s, wg1_s, u0_s, u1_s, v0_s, v1_s, xs_s, r2_s, e2_s, lim_s, e1_s,
                       u_sem, v_sem, *, eb, ne):
    e = pl.program_id(1)
    at_s, wg_s, u_s, v_s = (at0_s, at1_s), (wg0_s, wg1_s), (u0_s, u1_s), (v0_s, v1_s)

    def u_copy(block, slot):
        return pltpu.make_async_copy(u_hbm.at[pl.ds(pl.multiple_of(block * eb, eb), eb), :], u_s[slot], u_sem.at[slot])

    def v_copy(block, slot):
        return pltpu.make_async_copy(vt_hbm.at[block], v_s[slot], v_sem.at[slot])

    @pl.when(e == 0)
    def _():
        u_copy(0, 0).start()
        acc_ref[...] = jnp.zeros_like(acc_ref)
        for buf in (at0_s, at1_s, wg0_s, wg1_s, v0_s, v1_s):
            buf[...] = jnp.zeros_like(buf)
        xs_s[...] = xnt_ref[...]
        r2_s[...] = rank2_ref[...]
        e2_s[...] = e2_ref[...]
        lim_s[...] = lim1_ref[...]
        e1_s[...] = e1_ref[...]

    for par in range(2):
        @pl.when((e % 2 == par) & (e < ne))
        def _(par=par):
            u_copy(e, par).wait()

        @pl.when((e % 2 == par) & (e >= 2))
        def _(par=par):
            v_copy(e - 2, par).wait()

        @pl.when((e % 2 == par) & (e + 1 < ne))
        def _(par=par):
            u_copy(e + 1, 1 - par).start()

        @pl.when((e % 2 == par) & (e >= 1) & (e <= ne))
        def _(par=par):
            v_copy(e - 1, 1 - par).start()

    groups = eb // PEER_NKEYS
    half = PEER_NKEYS // 2
    tb = xs_s.shape[1]
    blk_b = jnp.clip(e - 1, 0, ne - 1)

    def weighting(par, ts):
        cols = slice(128 * ts, 128 * (ts + 1))
        for j0 in range(0, groups, 2):
            wt = [None, None]
            for h in range(PEER_HEADS):
                r2 = pltpu.bitcast(r2_s[h, half * ts:half * (ts + 1), :], BF16)
                w2 = pltpu.bitcast(e2_s[h, half * ts:half * (ts + 1), :], BF16)
                for dj in range(2):
                    i1 = blk_b * groups + j0 + dj
                    lim = jnp.broadcast_to(lim_s[h, ts, pl.ds(i1, 1), :].astype(BF16), (PEER_NKEYS, 128))
                    w1 = jnp.broadcast_to(e1_s[h, ts, pl.ds(i1, 1), :].astype(BF16), (PEER_NKEYS, 128))
                    w = jnp.where(r2 < lim, w2, 0.0) * w1
                    wt[dj] = w if wt[dj] is None else wt[dj] + w
            for dj in range(2):
                rows = slice(PEER_NKEYS * (j0 + dj), PEER_NKEYS * (j0 + dj + 1))
                wg_s[1 - par][rows, cols] = wt[dj] * jax.nn.gelu(at_s[1 - par][rows, cols]).astype(BF16)

    def stages(par):
        nq = tb // 128
        hw = tb // 2
        for q in range(nq):
            cols = slice(hw * (q % 2), hw * (q % 2 + 1))
            if q < nq // 2:
                at_s[par][:, cols] = _dot(u_s[par][...], xs_s[:, cols])
            else:
                acc_ref[:, cols] += _dot(v_s[par][...], wg_s[par][:, cols])
            weighting(par, q)

    for par in range(2):
        @pl.when(e % 2 == par)
        def _(par=par):
            stages(par)

    @pl.when(e == ne + 1)
    def _():
        h2 = h1_ref[...] + acc_ref[...].T
        gate = jax.nn.sigmoid(_dot((_rms(h2) * pgn_ref[...]).astype(BF16), pgw_ref[...]))
        out_ref[...] = h2 + _dot(p_ref[...].astype(BF16), pproj_ref[...]) * gate


def _peer_dense(xnt, u, vt, rank2, e2, lim1, e1, h1, p2, pproj, pgn, pgw, tb, eb):
    D, N = xnt.shape
    ne = u.shape[0] // eb
    full = lambda a: pl.BlockSpec(a.shape, lambda t, e: (0,) * a.ndim)
    sel_block = (PEER_HEADS, tb // 128, PEER_NKEYS, 128)
    sel_spec = pl.BlockSpec(sel_block, lambda t, e: (0, t, 0, 0))
    packed_block = (PEER_HEADS, tb // 128 * (PEER_NKEYS // 2), 128)
    packed_spec = pl.BlockSpec(packed_block, lambda t, e: (0, t, 0))
    rank2 = rank2.reshape(PEER_HEADS, -1, 128)
    e2 = e2.reshape(PEER_HEADS, -1, 128)
    any_spec = pl.BlockSpec(memory_space=pl.ANY)
    return pl.pallas_call(
        functools.partial(_peer_dense_kernel, eb=eb, ne=ne),
        grid=(N // tb, ne + 2),
        in_specs=[pl.BlockSpec((D, tb), lambda t, e: (0, t)), any_spec, any_spec,
                  packed_spec, packed_spec, sel_spec, sel_spec,
                  pl.BlockSpec((tb, D), lambda t, e: (t, 0)), pl.BlockSpec((tb, p2.shape[1]), lambda t, e: (t, 0)),
                  full(pproj), full(pgn), full(pgw)],
        out_specs=pl.BlockSpec((tb, D), lambda t, e: (t, 0)),
        out_shape=jax.ShapeDtypeStruct((N, D), F32),
        scratch_shapes=[pltpu.VMEM((D, tb), F32),
                        pltpu.VMEM((eb, tb), F32), pltpu.VMEM((eb, tb), F32),
                        pltpu.VMEM((eb, tb), BF16), pltpu.VMEM((eb, tb), BF16),
                        pltpu.VMEM((eb, D), BF16), pltpu.VMEM((eb, D), BF16),
                        pltpu.VMEM((D, eb), BF16), pltpu.VMEM((D, eb), BF16),
                        pltpu.VMEM((D, tb), BF16),
                        pltpu.VMEM(packed_block, jnp.uint32), pltpu.VMEM(packed_block, jnp.uint32),
                        pltpu.VMEM(sel_block, F32), pltpu.VMEM(sel_block, F32),
                        pltpu.SemaphoreType.DMA((2,)), pltpu.SemaphoreType.DMA((2,))],
        compiler_params=_params("arbitrary", "arbitrary"),
        name="peer_dense",
    )(xnt, u, vt, rank2, e2, lim1, e1, h1, p2, pproj, pgn, pgw)


def _block(n, pref):
    while n % pref:
        pref //= 2
    return pref


def kernel(x, p, mix_norm, w_in, hg_lb_logits, hg_out_norm, nsa_q_norm, nsa_k_norm, cmp_pe, cmp_w1, cmp_w2,
           nsa_out_norm, w_out, ffn_norm, peer_wq, peer_keys, peer_u, peer_v, ple_proj, ple_gate_norm, ple_gate_w):
    B, T, D = x.shape
    N = B * T
    x2 = x.reshape(N, D)
    nsa_w = NSA_HEADS * NSA_DH
    kvw = NSA_G * NSA_DH

    w = w_in[0]
    c0 = 4 * HG_WIDTH
    whg = w[:, :c0].astype(BF16)
    wq = w[:, c0:c0 + nsa_w].astype(BF16)
    wkv = w[:, c0 + nsa_w:c0 + nsa_w + 6 * kvw].astype(BF16)
    wg = w[:, c0 + nsa_w + 6 * kvw:]
    per_g = 3 * NSA_R
    wgate = jnp.zeros((D, 256), F32)
    for g in range(NSA_G):
        wgate = wgate.at[:, 128 * g:128 * g + per_g].set(wg[:, per_g * g:per_g * (g + 1)])
    wgate = wgate.astype(BF16)

    hg, qt, kcr, vcr, ks, vst, kw, vwt, gates_t = _inproj(
        x2, mix_norm, whg, wq, wkv, wgate, nsa_q_norm.reshape(NSA_DH, 1), nsa_k_norm[0], B, T, _block(T, 256))

    o_hg = _hgrn(hg.reshape(B, T, 4 * HG_WIDTH), hg_lb_logits, hg_out_norm, _block(T // HG_CHUNK, 4))

    ncp = T // CMP_STRIDE
    stride_rows = lambda a: a.reshape(B * NSA_G, ncp, CMP_STRIDE * NSA_DH)
    pe8 = jnp.broadcast_to(cmp_pe[0].reshape(2, 1, CMP_LEN * NSA_DH), (2, 8, CMP_LEN * NSA_DH))
    kc, vct = _compress(stride_rows(kcr), stride_rows(vcr), cmp_w1[0], cmp_w2[0], pe8, nsa_k_norm[0])
    kc = kc.reshape(B, NSA_G, ncp, NSA_DH)
    vct = vct.reshape(B, NSA_G, NSA_DH, ncp)

    o_nsa = _nsa(qt, gates_t, kc, vct, ks, vst, kw, vwt, nsa_out_norm)

    wo = w_out[0].astype(BF16)
    wqt = peer_wq[0].T.astype(BF16)
    keys = peer_keys[0].reshape(2 * PEER_HEADS, PEER_NKEYS, -1).astype(BF16)
    h1, xnt, rank2, e2, lim1, e1 = _peer_select(
        x2, o_hg.reshape(N, HG_WIDTH), o_nsa, wo[:HG_WIDTH], wo[HG_WIDTH:], ffn_norm, wqt, keys, _block(N, 256))

    eb = 512
    vt = peer_v[0].reshape(-1, eb, D).transpose(0, 2, 1).astype(BF16)
    out = _peer_dense(xnt, peer_u[0].astype(BF16), vt, rank2, e2, lim1, e1, h1,
                      p[0].reshape(N, -1), ple_proj[0].astype(BF16), ple_gate_norm, ple_gate_w[0].astype(BF16),
                      _block(N, 512), eb)
    return out.reshape(B, T, D)
```

```python
import functools

import numpy as np
import jax
import jax.numpy as jnp
from jax import lax
from jax.experimental import pallas as pl
from jax.experimental.pallas import tpu as pltpu

F32 = jnp.float32
BF16 = jnp.bfloat16
HIGHEST = lax.Precision.HIGHEST
EPS = 1e-6
NEG = -1e30
REMOVED = -3e38
MARK_STEP = 2e36
MARK_LIMIT = -2.5e38

HG_HEADS = 4
HG_D = 128
HG_CHUNK = 64
HG_WIDTH = HG_HEADS * HG_D
NSA_DH = 64
NSA_HEADS = 8
NSA_G = 2
NSA_R = NSA_HEADS // NSA_G
CMP_LEN = 32
CMP_STRIDE = 16
SLC_LEN = 64
SLC_TOPK = 16
WIN = 512
TQ = 128
SLC_CHUNK = 256
PEER_HEADS = 8
PEER_NKEYS = 128
PEER_TOPK = 16
VMEM_LIMIT = 56 * 1024 * 1024


def _dot(a, b, precision=None):
    return lax.dot_general(a, b, (((1,), (0,)), ((), ())), preferred_element_type=F32, precision=precision)


def _dot_nt(a, b):
    return lax.dot_general(a, b, (((1,), (1,)), ((), ())), preferred_element_type=F32)


def _dot_tn(a, b):
    return lax.dot_general(a, b, (((0,), (0,)), ((), ())), preferred_element_type=F32)


def _rms(x):
    return x * lax.rsqrt(jnp.mean(x * x, axis=-1, keepdims=True) + EPS)


def _params(*sem):
    return pltpu.CompilerParams(dimension_semantics=sem, vmem_limit_bytes=VMEM_LIMIT)


def _alibi_slope_row(g, width):
    lane_h = jnp.right_shift(lax.broadcasted_iota(jnp.int32, (1, width), 1), TQ.bit_length() - 1)
    row = jnp.zeros((1, width), F32)
    for h in range(NSA_R):
        row = jnp.where(lane_h == h, jnp.where(g == 0, 2.0 ** -(h + 1), 2.0 ** -(NSA_R + h + 1)).astype(F32), row)
    return row


def _inproj_kernel(x_ref, g_ref, whg_ref, wq_ref, wkv_ref, wgate_ref, qgc_ref, kg_ref,
                   hg_ref, qt_ref, kcr_ref, vcr_ref, ks_ref, vst_ref, kw_ref, vwt_ref, gate_ref, *, nt):
    a = (_rms(x_ref[...]) * g_ref[...]).astype(BF16)
    hg_ref[...] = _dot(a, whg_ref[...])
    tb = a.shape[0]
    nqb = tb // TQ
    width = NSA_R * TQ

    qt = _dot(a, wq_ref[...]).T
    aug_row = lax.broadcasted_iota(jnp.int32, (NSA_DH, width), 0)
    for g in range(NSA_G):
        slope = _alibi_slope_row(g, width)
        aug = jnp.where(aug_row == 0, slope, jnp.where(aug_row == 1, slope * TQ, 0.0)).astype(BF16)
        for j in range(nqb):
            qt_ref[0, g, j, NSA_DH:2 * NSA_DH, :] = aug
        for h in range(NSA_R):
            blk = qt[NSA_DH * (NSA_R * g + h):NSA_DH * (NSA_R * g + h + 1)]
            r = lax.rsqrt(jnp.mean(blk * blk, axis=0, keepdims=True) + EPS)
            qn = (blk * r * qgc_ref[...] * NSA_DH ** -0.5).astype(BF16)
            for j in range(nqb):
                qt_ref[0, g, j, 0:NSA_DH, TQ * h:TQ * (h + 1)] = qn[:, TQ * j:TQ * (j + 1)]

    kv = _dot(a, wkv_ref[...])
    kvt = kv.T
    pos = (pl.program_id(0) % nt) * tb + lax.broadcasted_iota(jnp.int32, (tb, NSA_DH), 0)
    aug_col = lax.broadcasted_iota(jnp.int32, (tb, NSA_DH), 1)
    pos_aug = jnp.where(aug_col == 0, jnp.bitwise_and(pos, TQ - 1),
                        jnp.where(aug_col == 1, jnp.right_shift(pos, TQ.bit_length() - 1), 0)).astype(F32)

    def part(j, g):
        return kv[:, 128 * j + NSA_DH * g:128 * j + NSA_DH * (g + 1)]

    def part_t(j, g):
        return kvt[128 * j + NSA_DH * g:128 * j + NSA_DH * (g + 1)]

    for g in range(NSA_G):
        kcr_ref[0, g] = part(0, g)
        vcr_ref[0, g] = part(1, g)
        ks_ref[0, g] = jnp.concatenate([_rms(part(2, g)) * kg_ref[1:2, :], pos_aug], axis=1).astype(BF16)
        vst_ref[0, g] = part_t(3, g).astype(BF16)
        kw_ref[0, g] = jnp.concatenate([_rms(part(4, g)) * kg_ref[2:3, :], pos_aug], axis=1).astype(BF16)
        vwt_ref[0, g] = part_t(5, g).astype(BF16)
    gate_ref[...] = jax.nn.sigmoid(_dot(a, wgate_ref[...])).T


def _inproj(x2, g, whg, wq, wkv, wgate, qgc, kg, B, T, tb):
    N, D = x2.shape
    nt = T // tb
    nqb = tb // TQ
    full = lambda shape: pl.BlockSpec(shape, lambda i: (0,) * len(shape))
    row_spec = lambda w: pl.BlockSpec((1, NSA_G, tb, w), lambda i: (i // nt, 0, i % nt, 0))
    col_spec = pl.BlockSpec((1, NSA_G, NSA_DH, tb), lambda i: (i // nt, 0, 0, i % nt))
    row_shape = lambda w, dt: jax.ShapeDtypeStruct((B, NSA_G, T, w), dt)
    col_shape = jax.ShapeDtypeStruct((B, NSA_G, NSA_DH, T), BF16)
    return pl.pallas_call(
        functools.partial(_inproj_kernel, nt=nt),
        grid=(N // tb,),
        in_specs=[pl.BlockSpec((tb, D), lambda i: (i, 0)), full(g.shape), full(whg.shape), full(wq.shape),
                  full(wkv.shape), full(wgate.shape), full(qgc.shape), full(kg.shape)],
        out_specs=[pl.BlockSpec((tb, 4 * HG_WIDTH), lambda i: (i, 0)),
                   pl.BlockSpec((1, NSA_G, nqb, 2 * NSA_DH, NSA_R * TQ), lambda i: (i // nt, 0, i % nt, 0, 0)),
                   row_spec(NSA_DH), row_spec(NSA_DH), row_spec(2 * NSA_DH), col_spec, row_spec(2 * NSA_DH), col_spec,
                   pl.BlockSpec((256, tb), lambda i: (0, i))],
        out_shape=[jax.ShapeDtypeStruct((N, 4 * HG_WIDTH), F32),
                   jax.ShapeDtypeStruct((B, NSA_G, T // TQ, 2 * NSA_DH, NSA_R * TQ), BF16),
                   row_shape(NSA_DH, F32), row_shape(NSA_DH, F32), row_shape(2 * NSA_DH, BF16), col_shape,
                   row_shape(2 * NSA_DH, BF16), col_shape,
                   jax.ShapeDtypeStruct((256, N), F32)],
        compiler_params=_params("parallel"),
        name="inproj",
    )(x2, g, whg, wq, wkv, wgate, qgc, kg)


def _hgrn_kernel(hg_ref, lbl_ref, gain_ref, o_ref, st_ref, *, nb, cpb):
    @pl.when(pl.program_id(0) == 0)
    def _():
        st_ref[...] = jnp.zeros_like(st_ref)

    logits = lbl_ref[...]
    ex = jnp.exp(logits - jnp.max(logits, axis=0, keepdims=True))
    lb_all = ex[0:1, :] / jnp.sum(ex, axis=0, keepdims=True)
    C = HG_CHUNK
    row = lax.broadcasted_iota(jnp.int32, (C, C), 0)
    col = lax.broadcasted_iota(jnp.int32, (C, C), 1)
    tril = row >= col
    crow = lax.broadcasted_iota(jnp.int32, (C, HG_D), 0)

    def chunk(c, carry):
        r0 = pl.multiple_of(c * C, C)
        rows = pl.ds(r0, C)
        for b in range(nb):
            for h in range(HG_HEADS):
                sl = slice(HG_D * h, HG_D * (h + 1))
                q, f, iv, g = [hg_ref[b, rows, j * HG_WIDTH + HG_D * h:j * HG_WIDTH + HG_D * (h + 1)] for j in range(4)]
                st = st_ref[b * HG_HEADS + h]
                lb = lb_all[:, sl]
                forget = lb + (1.0 - lb) * jax.nn.sigmoid(f)
                logf = jnp.log(forget)
                key = 1.0 - forget
                query = jax.nn.silu(q) * HG_D ** -0.5
                G = logf
                for d in (1, 2, 4, 8, 16, 32):
                    G = G + jnp.where(crow >= d, pltpu.roll(G, d, 0), 0.0)
                g_ref = G[C // 2 - 1:C // 2, :]
                g_last = G[C - 1:C, :]
                qa = (query * jnp.exp(G - g_ref)).astype(BF16)
                ka = (key * jnp.exp(g_ref - G)).astype(BF16)
                a = jnp.where(tril, _dot_nt(qa, ka), 0.0)
                vb = iv.astype(BF16)
                o = _dot(a.astype(BF16), vb)
                o = o + _dot_nt((query * jnp.exp(G)).astype(BF16), st.astype(BF16))
                kb = (key * jnp.exp(g_last - G)).astype(BF16)
                st_ref[b * HG_HEADS + h] = st * jnp.exp(g_last) + _dot_tn(vb, kb)
                on = _rms(o) * gain_ref[:, sl] * jax.nn.silu(g)
                o_ref[b, rows, sl] = on.astype(o_ref.dtype)
        return carry

    lax.fori_loop(0, cpb, chunk, 0)


def _hgrn(hg3, lb_logits, gain, cpb):
    B, T, W = hg3.shape
    rows = cpb * HG_CHUNK
    return pl.pallas_call(
        functools.partial(_hgrn_kernel, nb=B, cpb=cpb),
        grid=(T // rows,),
        in_specs=[pl.BlockSpec((B, rows, W), lambda i: (0, i, 0)),
                  pl.BlockSpec(lb_logits.shape, lambda i: (0, 0)),
                  pl.BlockSpec(gain.shape, lambda i: (0, 0))],
        out_specs=pl.BlockSpec((B, rows, HG_WIDTH), lambda i: (0, i, 0)),
        out_shape=jax.ShapeDtypeStruct((B, T, HG_WIDTH), BF16),
        scratch_shapes=[pltpu.VMEM((B * HG_HEADS, HG_D, HG_D), F32)],
        compiler_params=_params("arbitrary"),
        name="hgrn2",
    )(hg3, lb_logits, gain)


def _cmp_kernel(xk_ref, xv_ref, w1_ref, w2_ref, pe_ref, kg_ref, kc_ref, vct_ref):
    half = CMP_STRIDE * NSA_DH
    ys = []
    for idx, x_ref in enumerate((xk_ref, xv_ref)):
        x = x_ref[0]
        n = x.shape[0]
        w1 = w1_ref[idx]
        first = _dot(x, w1[:half], HIGHEST)
        second = _dot(x, w1[half:], HIGHEST)
        const = _dot(pe_ref[idx], w1, HIGHEST)[0:1, :]
        y = jax.nn.gelu(first + pltpu.roll(second, n - 1, 0) + const)
        ys.append(_dot(y, w2_ref[idx], HIGHEST))
    kc_ref[0] = (_rms(ys[0]) * kg_ref[0:1, :]).astype(BF16)
    vct_ref[0] = jnp.concatenate(ys, axis=1).T[NSA_DH:].astype(BF16)


def _compress(xk, xv, w1, w2, pe8, kg):
    BG, n, K = xk.shape
    full = lambda a: pl.BlockSpec(a.shape, lambda i: (0,) * a.ndim)
    blk = pl.BlockSpec((1, n, K), lambda i: (i, 0, 0))
    return pl.pallas_call(
        _cmp_kernel,
        grid=(BG,),
        in_specs=[blk, blk, full(w1), full(w2), full(pe8), full(kg)],
        out_specs=[pl.BlockSpec((1, n, NSA_DH), lambda i: (i, 0, 0)), pl.BlockSpec((1, NSA_DH, n), lambda i: (i, 0, 0))],
        out_shape=[jax.ShapeDtypeStruct((BG, n, NSA_DH), BF16), jax.ShapeDtypeStruct((BG, NSA_DH, n), BF16)],
        compiler_params=_params("parallel"),
        name="nsa_compress",
    )(xk, xv, w1, w2, pe8, kg)


def _nsa_kernel(qt_ref, gate_ref, kc_ref, vct_ref, ks_ref, vst_ref, kw_ref, vwt_ref, og_ref,
                o_ref, m_s, l_s, acc_s, sel_s, sa_s, sb_s, pa_s, pb_s, aa_s, ab_s, psum_s, ow_s):
    g = pl.program_id(1)
    i = pl.program_id(2)
    t0 = i * TQ
    width = NSA_R * TQ
    qa = qt_ref[0, 0, 0]
    q = qa[0:NSA_DH]
    slope_row = _alibi_slope_row(g, width)
    hcols = [slice(h * TQ, (h + 1) * TQ) for h in range(NSA_R)]

    ncp = kc_ref.shape[2]
    nn = lax.broadcasted_iota(jnp.int32, (ncp, TQ), 0)
    tt = lax.broadcasted_iota(jnp.int32, (ncp, TQ), 1)
    dist_c = t0 + tt - (nn * CMP_STRIDE + CMP_LEN - 1)
    valid_c = dist_c >= 0
    dist_cf = dist_c.astype(F32)
    s_c = _dot(kc_ref[0, 0], q)
    psum = jnp.zeros((ncp, TQ), F32)
    probs = []
    for h in range(NSA_R):
        s = jnp.where(valid_c, s_c[:, hcols[h]] - slope_row[:, hcols[h]] * dist_cf, NEG)
        m = jnp.max(s, axis=0, keepdims=True)
        m = jnp.where(m > 0.1 * NEG, m, 0.0)
        e = jnp.where(valid_c, jnp.exp(s - m), 0.0)
        p = e / jnp.maximum(jnp.sum(e, axis=0, keepdims=True), 1e-30)
        psum = psum + p
        probs.append(p.astype(BF16))
    o_cmp = _dot(vct_ref[0, 0], jnp.concatenate(probs, axis=1))

    wlen = WIN + TQ
    kw0 = pl.multiple_of(jnp.maximum(t0 - WIN, 0), TQ)
    wkey = lax.broadcasted_iota(jnp.int32, (wlen, TQ), 0)
    wdist = t0 + lax.broadcasted_iota(jnp.int32, (wlen, TQ), 1) - (kw0 + wkey)
    wneg = jnp.where(wdist >= 0, jnp.where(wdist < WIN, 0.0, NEG), NEG)
    sw = _dot(kw_ref[0, 0, pl.ds(kw0, wlen), :], qa) + jnp.concatenate([wneg] * NSA_R, axis=1)
    pw = jnp.exp(sw - jnp.max(sw, axis=0, keepdims=True))
    ow_s[...] = _dot(vwt_ref[0, 0, :, pl.ds(kw0, wlen)], pw.astype(BF16)) / jnp.sum(pw, axis=0, keepdims=True)

    ratio = SLC_LEN // CMP_STRIDE
    weights = np.convolve(np.ones(ratio), np.ones(CMP_LEN // CMP_STRIDE))
    nsb = ncp // ratio
    psum_s[0:ncp, :] = psum
    psum_s[ncp:, :] = jnp.zeros((psum_s.shape[0] - ncp, TQ), F32)
    imp = float(weights[0]) * psum_s[pl.ds(0, nsb, stride=ratio), :]
    for o in range(1, len(weights)):
        imp = imp + float(weights[o]) * psum_s[pl.ds(o, nsb, stride=ratio), :]
    if nsb < 128:
        imp = jnp.concatenate([imp, jnp.zeros((128 - nsb, TQ), F32)], axis=0)
    blk = lax.broadcasted_iota(jnp.int32, (128, TQ), 0).astype(F32)
    tcol = lax.broadcasted_iota(jnp.int32, (128, TQ), 1)
    cur = jnp.right_shift(t0 + tcol, SLC_LEN.bit_length() - 1).astype(F32)
    forced = (blk == 0.0) | (blk == cur) | (blk == cur - 1.0)
    v = jnp.where(forced, -NEG, jnp.where(blk <= cur, imp, NEG))
    sel = jnp.zeros((128, TQ), F32)
    for _ in range(SLC_TOPK):
        m = jnp.max(v, axis=0, keepdims=True)
        idx = jnp.min(jnp.where(v == m, blk, 128.0), axis=0, keepdims=True)
        pick = blk == idx
        sel = jnp.where(pick, 1.0, sel)
        v = jnp.where(pick, REMOVED, v)
    sel_s[...] = sel

    m_s[...] = jnp.full(m_s.shape, NEG, F32)
    l_s[...] = jnp.zeros(l_s.shape, F32)
    acc_s[...] = jnp.zeros(acc_s.shape, F32)
    keyr = lax.broadcasted_iota(jnp.int32, (SLC_CHUNK, TQ), 0)
    tokc = lax.broadcasted_iota(jnp.int32, (SLC_CHUNK, TQ), 1)
    blocks_per_chunk = SLC_CHUNK // SLC_LEN

    last_k0 = ks_ref.shape[2] - SLC_CHUNK

    def key_chunk(c):
        k0 = pl.multiple_of(jnp.minimum(c * SLC_CHUNK, last_k0), SLC_CHUNK)
        return ks_ref[0, 0, pl.ds(k0, SLC_CHUNK), :]

    def value_chunk(c):
        return vst_ref[0, 0, :, pl.ds(pl.multiple_of(c * SLC_CHUNK, SLC_CHUNK), SLC_CHUNK)]

    def softmax_step(s_ref, p_ref, a_ref, c):
        k0 = c * SLC_CHUNK
        selx = jnp.concatenate(
            [jnp.broadcast_to(sel_s[pl.ds(blocks_per_chunk * c + j, 1), :], (SLC_LEN, TQ)) for j in range(blocks_per_chunk)],
            axis=0)
        negm = jnp.where(selx > 0.5, jnp.where(k0 + keyr <= t0 + tokc, 0.0, NEG), NEG)
        s = s_ref[...] + jnp.concatenate([negm] * NSA_R, axis=1)
        m_old = m_s[...]
        m_new = jnp.maximum(m_old, jnp.max(s, axis=0, keepdims=True))
        alpha = jnp.exp(m_old - m_new)
        p = jnp.exp(s - m_new)
        l_s[...] = alpha * l_s[...] + jnp.sum(p, axis=0, keepdims=True)
        m_s[...] = m_new
        a_ref[...] = alpha
        p_ref[...] = p.astype(BF16)

    def accumulate(p_ref, a_ref, c):
        acc_s[...] = a_ref[...] * acc_s[...] + _dot(value_chunk(c), p_ref[...])

    sa_s[...] = _dot(key_chunk(0), qa)
    pb_s[...] = jnp.zeros_like(pb_s)
    ab_s[...] = jnp.ones_like(ab_s)

    def slc_pair(c, carry):
        sb_s[...] = _dot(key_chunk(2 * c + 1), qa)
        softmax_step(sa_s, pa_s, aa_s, 2 * c)
        accumulate(pb_s, ab_s, jnp.maximum(2 * c - 1, 0))
        sa_s[...] = _dot(key_chunk(2 * c + 2), qa)
        softmax_step(sb_s, pb_s, ab_s, 2 * c + 1)
        accumulate(pa_s, aa_s, 2 * c)
        return carry

    n_pairs = lax.shift_right_logical(t0 + TQ + 2 * SLC_CHUNK - 1, SLC_CHUNK.bit_length())
    lax.fori_loop(0, n_pairs, slc_pair, 0)
    accumulate(pb_s, ab_s, 2 * n_pairs - 1)
    o_s = acc_s[...] / l_s[...]

    o_w = ow_s[...]
    gts = gate_ref[...]
    outs = []
    for h in range(NSA_R):
        cols = hcols[h]
        o = (gts[3 * h:3 * h + 1, :] * o_cmp[:, cols] + gts[3 * h + 1:3 * h + 2, :] * o_s[:, cols]
             + gts[3 * h + 2:3 * h + 3, :] * o_w[:, cols])
        outs.append(o * lax.rsqrt(jnp.mean(o * o, axis=0, keepdims=True) + EPS))
    o_ref[...] = (jnp.concatenate(outs, axis=0).T * og_ref[...]).astype(o_ref.dtype)


def _nsa(qt, gates_t, kc, vct, ks, vst, kw, vwt, og):
    B, _, nq, _, width = qt.shape
    T = nq * TQ
    ncp = kc.shape[2]
    rows = pl.BlockSpec((1, 1, T, 2 * NSA_DH), lambda b, g, i: (b, g, 0, 0))
    cols = pl.BlockSpec((1, 1, NSA_DH, T), lambda b, g, i: (b, g, 0, 0))
    ow = NSA_R * NSA_DH
    return pl.pallas_call(
        _nsa_kernel,
        grid=(B, NSA_G, nq),
        in_specs=[pl.BlockSpec((1, 1, 1, 2 * NSA_DH, width), lambda b, g, i: (b, g, i, 0, 0)),
                  pl.BlockSpec((128, TQ), lambda b, g, i: (g, b * nq + i)),
                  pl.BlockSpec((1, 1, ncp, NSA_DH), lambda b, g, i: (b, g, 0, 0)),
                  pl.BlockSpec((1, 1, NSA_DH, ncp), lambda b, g, i: (b, g, 0, 0)),
                  rows, cols, rows, cols,
                  pl.BlockSpec((1, ow), lambda b, g, i: (0, g))],
        out_specs=pl.BlockSpec((TQ, ow), lambda b, g, i: (b * nq + i, g)),
        out_shape=jax.ShapeDtypeStruct((B * T, NSA_G * ow), BF16),
        scratch_shapes=[pltpu.VMEM((1, width), F32), pltpu.VMEM((1, width), F32),
                        pltpu.VMEM((NSA_DH, width), F32), pltpu.VMEM((128, TQ), F32),
                        pltpu.VMEM((SLC_CHUNK, width), F32), pltpu.VMEM((SLC_CHUNK, width), F32),
                        pltpu.VMEM((SLC_CHUNK, width), BF16), pltpu.VMEM((SLC_CHUNK, width), BF16),
                        pltpu.VMEM((1, width), F32), pltpu.VMEM((1, width), F32),
                        pltpu.VMEM((ncp + 8, TQ), F32), pltpu.VMEM((NSA_DH, width), F32)],
        compiler_params=_params("parallel", "parallel", "arbitrary"),
        name="nsa_attn",
    )(qt, gates_t, kc, vct, ks, vst, kw, vwt, og)


def _peer_sel_kernel(x_ref, ohg_ref, onsa_ref, wo1_ref, wo2_ref, fg_ref, wqt_ref, keys_ref,
                     h1_ref, xnt_ref, rank2_ref, e2_ref, lim1_ref, e1_ref, qt_s, r1_s, r2_s, lim_s, top_s):
    h1 = x_ref[...] + _dot(ohg_ref[...], wo1_ref[...]) + _dot(onsa_ref[...], wo2_ref[...])
    h1_ref[...] = h1
    xnt = (_rms(h1) * fg_ref[...]).T.astype(BF16)
    xnt_ref[...] = xnt
    qt_s[...] = _dot(wqt_ref[...], xnt)
    tb = xnt.shape[1]
    K = PEER_TOPK
    rowi = lax.broadcasted_iota(jnp.int32, (PEER_NKEYS, tb), 0).astype(F32)
    k1i = lax.broadcasted_iota(jnp.int32, (K, tb), 0).astype(F32)
    r = lax.broadcasted_iota(jnp.int32, (K + 7 * 8 + 8, tb), 0)
    ci = jnp.where(r < K, r, jnp.where(r < K + 56, K * (1 + jnp.right_shift(r - K, 3)) + jnp.bitwise_and(r - K, 7),
                                       K * (r - (K + 56) + 8))).astype(F32)

    mark = [REMOVED + k * MARK_STEP for k in range(K)]

    def topk_exact(s):
        rank = jnp.full(s.shape, float(PEER_NKEYS), F32)
        tops = []
        for k in range(K):
            m = jnp.max(s, axis=0, keepdims=True)
            idx = jnp.min(jnp.where(s == m, rowi, float(PEER_NKEYS)), axis=0, keepdims=True)
            pick = rowi == idx
            rank = jnp.where(pick, float(k), rank)
            s = jnp.where(pick, REMOVED, s)
            tops.append(m)
        return rank, jnp.concatenate(tops, axis=0)

    def topk_fast(s):
        tops = []
        for k in range(K):
            m = jnp.max(s, axis=0, keepdims=True)
            s = jnp.where(s == m, mark[k], s)
            tops.append(m)
        taken = s < MARK_LIMIT
        rank = jnp.where(taken, jnp.floor((s - mark[0]) * (1.0 / MARK_STEP) + 0.5), float(PEER_NKEYS))
        count = jnp.sum(jnp.where(taken, 1.0, 0.0), axis=0, keepdims=True)
        return rank, jnp.concatenate(tops, axis=0), count

    def candidates(t1, t2):
        return jnp.concatenate([t1[0:1, :] + t2] + [t1[k:k + 1, :] + t2[0:8, :] for k in range(1, 8)]
                               + [t1[8:K, :] + t2[0:1, :]], axis=0)

    def pairs_exact(t1, t2):
        cand = candidates(t1, t2)
        m0 = t1[0:1, :] + t2[0:1, :]
        lim = jnp.zeros((K, tb), F32)
        z = jnp.zeros((1, tb), F32)
        for k in range(K):
            m = jnp.max(cand, axis=0, keepdims=True)
            idx = jnp.min(jnp.where(cand == m, ci, float(K * K)), axis=0, keepdims=True)
            cand = jnp.where(ci == idx, REMOVED, cand)
            lim = lim + jnp.where(k1i == jnp.floor(idx * (1.0 / K)), 1.0, 0.0)
            z = z + jnp.exp(m - m0)
        return lim, z

    def pairs_fast(t1, t2):
        cand = candidates(t1, t2)
        m0 = t1[0:1, :] + t2[0:1, :]
        z = jnp.zeros((1, tb), F32)
        for k in range(K):
            m = jnp.max(cand, axis=0, keepdims=True)
            cand = jnp.where(cand == m, REMOVED, cand)
            z = z + jnp.exp(m - m0)
        taken = jnp.where(cand < MARK_LIMIT, 1.0, 0.0)
        rowsum = lambda a, b: jnp.sum(taken[a:b, :], axis=0, keepdims=True)
        lim = jnp.concatenate([rowsum(0, K)] + [rowsum(K + 8 * j, K + 8 * (j + 1)) for j in range(7)]
                              + [taken[K + 56:, :]], axis=0)
        return lim, z, jnp.sum(taken, axis=0, keepdims=True)

    def store_selection(r1, r2, t1, t2, lim, z):
        r1_s[...] = r1
        r2_s[...] = r2
        lim_s[...] = lim
        top_s[0:1, :] = t1[0:1, :]
        top_s[1:2, :] = t2[0:1, :]
        top_s[2:3, :] = z

    def head(h, carry):
        r = pl.multiple_of(h * 2 * PEER_NKEYS, 2 * PEER_NKEYS)
        s1 = _dot(keys_ref[2 * h], qt_s[pl.ds(r, PEER_NKEYS), :].astype(BF16))
        s2 = _dot(keys_ref[2 * h + 1], qt_s[pl.ds(r + PEER_NKEYS, PEER_NKEYS), :].astype(BF16))
        r1, t1, n1 = topk_fast(s1)
        r2, t2, n2 = topk_fast(s2)
        lim, z, n3 = pairs_fast(t1, t2)
        store_selection(r1, r2, t1, t2, lim, z)
        off_count = jnp.abs(n1 - K) + jnp.abs(n2 - K) + jnp.abs(n3 - K)

        @pl.when(jnp.max(off_count) > 0.5)
        def _():
            r1, t1 = topk_exact(s1)
            r2, t2 = topk_exact(s2)
            lim, z = pairs_exact(t1, t2)
            store_selection(r1, r2, t1, t2, lim, z)

        r1 = r1_s[...]
        lim = lim_s[...]
        rank_bits = r1.astype(jnp.int32)
        level = [lim[k:k + 1, :] for k in range(K)]
        for bit in range(K.bit_length() - 1):
            odd = jnp.bitwise_and(rank_bits, 1 << bit) != 0
            level = [jnp.where(odd, level[2 * i + 1], level[2 * i]) for i in range(len(level) // 2)]
        lim1 = jnp.where(r1 < float(K), level[0], 0.0)
        e1 = jnp.exp(s1 - top_s[0:1, :])
        e2 = jnp.exp(s2 - top_s[1:2, :]) / top_s[2:3, :]
        r2 = r2_s[...]
        for ts in range(tb // 128):
            cols = slice(128 * ts, 128 * (ts + 1))
            rank2_ref[h, ts] = pltpu.bitcast(r2[:, cols].astype(BF16), jnp.uint32)
            lim1_ref[h, ts] = lim1[:, cols]
            e1_ref[h, ts] = e1[:, cols]
            e2_ref[h, ts] = pltpu.bitcast(e2[:, cols].astype(BF16), jnp.uint32)
        return carry

    lax.fori_loop(0, PEER_HEADS, head, 0)


def _peer_select(x2, ohg, onsa, wo1, wo2, fg, wqt, keys, tb):
    N, D = x2.shape
    full = lambda a: pl.BlockSpec(a.shape, lambda i: (0,) * a.ndim)
    sel_spec = pl.BlockSpec((PEER_HEADS, tb // 128, PEER_NKEYS, 128), lambda i: (0, i, 0, 0))
    sel_shape = jax.ShapeDtypeStruct((PEER_HEADS, N // 128, PEER_NKEYS, 128), F32)
    packed_spec = pl.BlockSpec((PEER_HEADS, tb // 128, PEER_NKEYS // 2, 128), lambda i: (0, i, 0, 0))
    packed_shape = jax.ShapeDtypeStruct((PEER_HEADS, N // 128, PEER_NKEYS // 2, 128), jnp.uint32)
    return pl.pallas_call(
        _peer_sel_kernel,
        grid=(N // tb,),
        in_specs=[pl.BlockSpec((tb, D), lambda i: (i, 0)), pl.BlockSpec((tb, ohg.shape[1]), lambda i: (i, 0)),
                  pl.BlockSpec((tb, onsa.shape[1]), lambda i: (i, 0)),
                  full(wo1), full(wo2), full(fg), full(wqt), full(keys)],
        out_specs=[pl.BlockSpec((tb, D), lambda i: (i, 0)), pl.BlockSpec((D, tb), lambda i: (0, i)),
                   packed_spec, packed_spec, sel_spec, sel_spec],
        out_shape=[jax.ShapeDtypeStruct((N, D), F32), jax.ShapeDtypeStruct((D, N), BF16),
                   packed_shape, packed_shape, sel_shape, sel_shape],
        scratch_shapes=[pltpu.VMEM((wqt.shape[0], tb), F32), pltpu.VMEM((PEER_NKEYS, tb), F32),
                        pltpu.VMEM((PEER_NKEYS, tb), F32), pltpu.VMEM((PEER_TOPK, tb), F32), pltpu.VMEM((8, tb), F32)],
        compiler_params=_params("parallel"),
        name="peer_select",
    )(x2, ohg, onsa, wo1, wo2, fg, wqt, keys)


def _peer_dense_kernel(xnt_ref, u_hbm, vt_hbm, rank2_ref, e2_ref, lim1_ref, e1_ref, h1_ref, p_ref,
                       pproj_ref, pgn_ref, pgw_ref, out_ref,
                       acc_ref, at0_s, at1_s, wg0_s, wg1_s, u0_s, u1_s, v0_s, v1_s, xs_s, r2_s, e2_s, lim_s, e1_s,
                       u_sem, v_sem, *, eb, ne):
    e = pl.program_id(1)
    at_s, wg_s, u_s, v_s = (at0_s, at1_s), (wg0_s, wg1_s), (u0_s, u1_s), (v0_s, v1_s)

    def u_copy(block, slot):
        return pltpu.make_async_copy(u_hbm.at[pl.ds(pl.multiple_of(block * eb, eb), eb), :], u_s[slot], u_sem.at[slot])

    def v_copy(block, slot):
        return pltpu.make_async_copy(vt_hbm.at[block], v_s[slot], v_sem.at[slot])

    @pl.when(e == 0)
    def _():
        u_copy(0, 0).start()
        acc_ref[...] = jnp.zeros_like(acc_ref)
        for buf in (at0_s, at1_s, wg0_s, wg1_s, v0_s, v1_s):
            buf[...] = jnp.zeros_like(buf)
        xs_s[...] = xnt_ref[...]
        r2_s[...] = rank2_ref[...]
        e2_s[...] = e2_ref[...]
        lim_s[...] = lim1_ref[...]
        e1_s[...] = e1_ref[...]

    for par in range(2):
        @pl.when((e % 2 == par) & (e < ne))
        def _(par=par):
            u_copy(e, par).wait()

        @pl.when((e % 2 == par) & (e >= 2))
        def _(par=par):
            v_copy(e - 2, par).wait()

        @pl.when((e % 2 == par) & (e + 1 < ne))
        def _(par=par):
            u_copy(e + 1, 1 - par).start()

        @pl.when((e % 2 == par) & (e >= 1) & (e <= ne))
        def _(par=par):
            v_copy(e - 1, 1 - par).start()

    groups = eb // PEER_NKEYS
    half = PEER_NKEYS // 2
    tb = xs_s.shape[1]
    blk_b = jnp.clip(e - 1, 0, ne - 1)

    def weighting(par, ts):
        cols = slice(128 * ts, 128 * (ts + 1))
        for j0 in range(0, groups, 2):
            wt = [None, None]
            for h in range(PEER_HEADS):
                r2 = pltpu.bitcast(r2_s[h, half * ts:half * (ts + 1), :], BF16)
                w2 = pltpu.bitcast(e2_s[h, half * ts:half * (ts + 1), :], BF16)
                for dj in range(2):
                    i1 = blk_b * groups + j0 + dj
                    lim = jnp.broadcast_to(lim_s[h, ts, pl.ds(i1, 1), :].astype(BF16), (PEER_NKEYS, 128))
                    w1 = jnp.broadcast_to(e1_s[h, ts, pl.ds(i1, 1), :].astype(BF16), (PEER_NKEYS, 128))
                    w = jnp.where(r2 < lim, w2, 0.0) * w1
                    wt[dj] = w if wt[dj] is None else wt[dj] + w
            for dj in range(2):
                rows = slice(PEER_NKEYS * (j0 + dj), PEER_NKEYS * (j0 + dj + 1))
                wg_s[1 - par][rows, cols] = wt[dj] * jax.nn.gelu(at_s[1 - par][rows, cols]).astype(BF16)

    def stages(par):
        nq = tb // 128
        hw = tb // 2
        for q in range(nq):
            cols = slice(hw * (q % 2), hw * (q % 2 + 1))
            if q < nq // 2:
                at_s[par][:, cols] = _dot(u_s[par][...], xs_s[:, cols])
            else:
                acc_ref[:, cols] += _dot(v_s[par][...], wg_s[par][:, cols])
            weighting(par, q)

    for par in range(2):
        @pl.when(e % 2 == par)
        def _(par=par):
            stages(par)

    @pl.when(e == ne + 1)
    def _():
        h2 = h1_ref[...] + acc_ref[...].T
        gate = jax.nn.sigmoid(_dot((_rms(h2) * pgn_ref[...]).astype(BF16), pgw_ref[...]))
        out_ref[...] = h2 + _dot(p_ref[...].astype(BF16), pproj_ref[...]) * gate


def _peer_dense(xnt, u, vt, rank2, e2, lim1, e1, h1, p2, pproj, pgn, pgw, tb, eb):
    D, N = xnt.shape
    ne = u.shape[0] // eb
    full = lambda a: pl.BlockSpec(a.shape, lambda t, e: (0,) * a.ndim)
    sel_block = (PEER_HEADS, tb // 128, PEER_NKEYS, 128)
    sel_spec = pl.BlockSpec(sel_block, lambda t, e: (0, t, 0, 0))
    packed_block = (PEER_HEADS, tb // 128 * (PEER_NKEYS // 2), 128)
    packed_spec = pl.BlockSpec(packed_block, lambda t, e: (0, t, 0))
    rank2 = rank2.reshape(PEER_HEADS, -1, 128)
    e2 = e2.reshape(PEER_HEADS, -1, 128)
    any_spec = pl.BlockSpec(memory_space=pl.ANY)
    return pl.pallas_call(
        functools.partial(_peer_dense_kernel, eb=eb, ne=ne),
        grid=(N // tb, ne + 2),
        in_specs=[pl.BlockSpec((D, tb), lambda t, e: (0, t)), any_spec, any_spec,
                  packed_spec, packed_spec, sel_spec, sel_spec,
                  pl.BlockSpec((tb, D), lambda t, e: (t, 0)), pl.BlockSpec((tb, p2.shape[1]), lambda t, e: (t, 0)),
                  full(pproj), full(pgn), full(pgw)],
        out_specs=pl.BlockSpec((tb, D), lambda t, e: (t, 0)),
        out_shape=jax.ShapeDtypeStruct((N, D), F32),
        scratch_shapes=[pltpu.VMEM((D, tb), F32),
                        pltpu.VMEM((eb, tb), F32), pltpu.VMEM((eb, tb), F32),
                        pltpu.VMEM((eb, tb), BF16), pltpu.VMEM((eb, tb), BF16),
                        pltpu.VMEM((eb, D), BF16), pltpu.VMEM((eb, D), BF16),
                        pltpu.VMEM((D, eb), BF16), pltpu.VMEM((D, eb), BF16),
                        pltpu.VMEM((D, tb), BF16),
                        pltpu.VMEM(packed_block, jnp.uint32), pltpu.VMEM(packed_block, jnp.uint32),
                        pltpu.VMEM(sel_block, F32), pltpu.VMEM(sel_block, F32),
                        pltpu.SemaphoreType.DMA((2,)), pltpu.SemaphoreType.DMA((2,))],
        compiler_params=_params("arbitrary", "arbitrary"),
        name="peer_dense",
    )(xnt, u, vt, rank2, e2, lim1, e1, h1, p2, pproj, pgn, pgw)


def _block(n, pref):
    while n % pref:
        pref //= 2
    return pref


def kernel(x, p, mix_norm, w_in, hg_lb_logits, hg_out_norm, nsa_q_norm, nsa_k_norm, cmp_pe, cmp_w1, cmp_w2,
           nsa_out_norm, w_out, ffn_norm, peer_wq, peer_keys, peer_u, peer_v, ple_proj, ple_gate_norm, ple_gate_w):
    B, T, D = x.shape
    N = B * T
    x2 = x.reshape(N, D)
    nsa_w = NSA_HEADS * NSA_DH
    kvw = NSA_G * NSA_DH

    w = w_in[0]
    c0 = 4 * HG_WIDTH
    whg = w[:, :c0].astype(BF16)
    wq = w[:, c0:c0 + nsa_w].astype(BF16)
    wkv = w[:, c0 + nsa_w:c0 + nsa_w + 6 * kvw].astype(BF16)
    wg = w[:, c0 + nsa_w + 6 * kvw:]
    per_g = 3 * NSA_R
    wgate = jnp.zeros((D, 256), F32)
    for g in range(NSA_G):
        wgate = wgate.at[:, 128 * g:128 * g + per_g].set(wg[:, per_g * g:per_g * (g + 1)])
    wgate = wgate.astype(BF16)

    hg, qt, kcr, vcr, ks, vst, kw, vwt, gates_t = _inproj(
        x2, mix_norm, whg, wq, wkv, wgate, nsa_q_norm.reshape(NSA_DH, 1), nsa_k_norm[0], B, T, _block(T, 256))

    o_hg = _hgrn(hg.reshape(B, T, 4 * HG_WIDTH), hg_lb_logits, hg_out_norm, _block(T // HG_CHUNK, 4))

    ncp = T // CMP_STRIDE
    stride_rows = lambda a: a.reshape(B * NSA_G, ncp, CMP_STRIDE * NSA_DH)
    pe8 = jnp.broadcast_to(cmp_pe[0].reshape(2, 1, CMP_LEN * NSA_DH), (2, 8, CMP_LEN * NSA_DH))
    kc, vct = _compress(stride_rows(kcr), stride_rows(vcr), cmp_w1[0], cmp_w2[0], pe8, nsa_k_norm[0])
    kc = kc.reshape(B, NSA_G, ncp, NSA_DH)
    vct = vct.reshape(B, NSA_G, NSA_DH, ncp)

    o_nsa = _nsa(qt, gates_t, kc, vct, ks, vst, kw, vwt, nsa_out_norm)

    wo = w_out[0].astype(BF16)
    wqt = peer_wq[0].T.astype(BF16)
    keys = peer_keys[0].reshape(2 * PEER_HEADS, PEER_NKEYS, -1).astype(BF16)
    h1, xnt, rank2, e2, lim1, e1 = _peer_select(
        x2, o_hg.reshape(N, HG_WIDTH), o_nsa, wo[:HG_WIDTH], wo[HG_WIDTH:], ffn_norm, wqt, keys, _block(N, 512))

    eb = 512
    vt = peer_v[0].reshape(-1, eb, D).transpose(0, 2, 1).astype(BF16)
    out = _peer_dense(xnt, peer_u[0].astype(BF16), vt, rank2, e2, lim1, e1, h1,
                      p[0].reshape(N, -1), ple_proj[0].astype(BF16), ple_gate_norm, ple_gate_w[0].astype(BF16),
                      _block(N, 512), eb)
    return out.reshape(B, T, D)
```

```python
import functools

import numpy as np
import jax
import jax.numpy as jnp
from jax import lax
from jax.experimental import pallas as pl
from jax.experimental.pallas import tpu as pltpu

F32 = jnp.float32
BF16 = jnp.bfloat16
HIGHEST = lax.Precision.HIGHEST
EPS = 1e-6
NEG = -1e30
REMOVED = -3e38
MARK_STEP = 2e36
MARK_LIMIT = -2.5e38

HG_HEADS = 4
HG_D = 128
HG_CHUNK = 64
HG_WIDTH = HG_HEADS * HG_D
NSA_DH = 64
NSA_HEADS = 8
NSA_G = 2
NSA_R = NSA_HEADS // NSA_G
CMP_LEN = 32
CMP_STRIDE = 16
SLC_LEN = 64
SLC_TOPK = 16
WIN = 512
TQ = 128
SLC_CHUNK = 256
PEER_HEADS = 8
PEER_NKEYS = 128
PEER_TOPK = 16
VMEM_LIMIT = 56 * 1024 * 1024


def _dot(a, b, precision=None):
    return lax.dot_general(a, b, (((1,), (0,)), ((), ())), preferred_element_type=F32, precision=precision)


def _dot_nt(a, b):
    return lax.dot_general(a, b, (((1,), (1,)), ((), ())), preferred_element_type=F32)


def _dot_tn(a, b):
    return lax.dot_general(a, b, (((0,), (0,)), ((), ())), preferred_element_type=F32)


def _rms(x):
    return x * lax.rsqrt(jnp.mean(x * x, axis=-1, keepdims=True) + EPS)


def _params(*sem):
    return pltpu.CompilerParams(dimension_semantics=sem, vmem_limit_bytes=VMEM_LIMIT)


def _alibi_slope_row(g, width):
    lane_h = jnp.right_shift(lax.broadcasted_iota(jnp.int32, (1, width), 1), TQ.bit_length() - 1)
    row = jnp.zeros((1, width), F32)
    for h in range(NSA_R):
        row = jnp.where(lane_h == h, jnp.where(g == 0, 2.0 ** -(h + 1), 2.0 ** -(NSA_R + h + 1)).astype(F32), row)
    return row


def _inproj_kernel(x_ref, g_ref, whg_ref, wq_ref, wkv_ref, wgate_ref, qgc_ref, kg_ref,
                   hg_ref, qt_ref, kcr_ref, vcr_ref, ks_ref, vst_ref, kw_ref, vwt_ref, gate_ref, *, nt):
    a = (_rms(x_ref[...]) * g_ref[...]).astype(BF16)
    hg_ref[...] = _dot(a, whg_ref[...])
    tb = a.shape[0]
    nqb = tb // TQ
    width = NSA_R * TQ

    qt = _dot(a, wq_ref[...]).T
    aug_row = lax.broadcasted_iota(jnp.int32, (NSA_DH, width), 0)
    for g in range(NSA_G):
        slope = _alibi_slope_row(g, width)
        aug = jnp.where(aug_row == 0, slope, jnp.where(aug_row == 1, slope * TQ, 0.0)).astype(BF16)
        for j in range(nqb):
            qt_ref[0, g, j, NSA_DH:2 * NSA_DH, :] = aug
        for h in range(NSA_R):
            blk = qt[NSA_DH * (NSA_R * g + h):NSA_DH * (NSA_R * g + h + 1)]
            r = lax.rsqrt(jnp.mean(blk * blk, axis=0, keepdims=True) + EPS)
            qn = (blk * r * qgc_ref[...] * NSA_DH ** -0.5).astype(BF16)
            for j in range(nqb):
                qt_ref[0, g, j, 0:NSA_DH, TQ * h:TQ * (h + 1)] = qn[:, TQ * j:TQ * (j + 1)]

    kv = _dot(a, wkv_ref[...])
    kvt = kv.T
    pos = (pl.program_id(0) % nt) * tb + lax.broadcasted_iota(jnp.int32, (tb, NSA_DH), 0)
    aug_col = lax.broadcasted_iota(jnp.int32, (tb, NSA_DH), 1)
    pos_aug = jnp.where(aug_col == 0, jnp.bitwise_and(pos, TQ - 1),
                        jnp.where(aug_col == 1, jnp.right_shift(pos, TQ.bit_length() - 1), 0)).astype(F32)

    def part(j, g):
        return kv[:, 128 * j + NSA_DH * g:128 * j + NSA_DH * (g + 1)]

    def part_t(j, g):
        return kvt[128 * j + NSA_DH * g:128 * j + NSA_DH * (g + 1)]

    for g in range(NSA_G):
        kcr_ref[0, g] = part(0, g)
        vcr_ref[0, g] = part(1, g)
        ks_ref[0, g] = jnp.concatenate([_rms(part(2, g)) * kg_ref[1:2, :], pos_aug], axis=1).astype(BF16)
        vst_ref[0, g] = part_t(3, g).astype(BF16)
        kw_ref[0, g] = jnp.concatenate([_rms(part(4, g)) * kg_ref[2:3, :], pos_aug], axis=1).astype(BF16)
        vwt_ref[0, g] = part_t(5, g).astype(BF16)
    gate_ref[...] = jax.nn.sigmoid(_dot(a, wgate_ref[...])).T


def _inproj(x2, g, whg, wq, wkv, wgate, qgc, kg, B, T, tb):
    N, D = x2.shape
    nt = T // tb
    nqb = tb // TQ
    full = lambda shape: pl.BlockSpec(shape, lambda i: (0,) * len(shape))
    row_spec = lambda w: pl.BlockSpec((1, NSA_G, tb, w), lambda i: (i // nt, 0, i % nt, 0))
    col_spec = pl.BlockSpec((1, NSA_G, NSA_DH, tb), lambda i: (i // nt, 0, 0, i % nt))
    row_shape = lambda w, dt: jax.ShapeDtypeStruct((B, NSA_G, T, w), dt)
    col_shape = jax.ShapeDtypeStruct((B, NSA_G, NSA_DH, T), BF16)
    return pl.pallas_call(
        functools.partial(_inproj_kernel, nt=nt),
        grid=(N // tb,),
        in_specs=[pl.BlockSpec((tb, D), lambda i: (i, 0)), full(g.shape), full(whg.shape), full(wq.shape),
                  full(wkv.shape), full(wgate.shape), full(qgc.shape), full(kg.shape)],
        out_specs=[pl.BlockSpec((tb, 4 * HG_WIDTH), lambda i: (i, 0)),
                   pl.BlockSpec((1, NSA_G, nqb, 2 * NSA_DH, NSA_R * TQ), lambda i: (i // nt, 0, i % nt, 0, 0)),
                   row_spec(NSA_DH), row_spec(NSA_DH), row_spec(2 * NSA_DH), col_spec, row_spec(2 * NSA_DH), col_spec,
                   pl.BlockSpec((256, tb), lambda i: (0, i))],
        out_shape=[jax.ShapeDtypeStruct((N, 4 * HG_WIDTH), F32),
                   jax.ShapeDtypeStruct((B, NSA_G, T // TQ, 2 * NSA_DH, NSA_R * TQ), BF16),
                   row_shape(NSA_DH, F32), row_shape(NSA_DH, F32), row_shape(2 * NSA_DH, BF16), col_shape,
                   row_shape(2 * NSA_DH, BF16), col_shape,
                   jax.ShapeDtypeStruct((256, N), F32)],
        compiler_params=_params("parallel"),
        name="inproj",
    )(x2, g, whg, wq, wkv, wgate, qgc, kg)


def _hgrn_kernel(hg_ref, lbl_ref, gain_ref, o_ref, st_ref, *, nb, cpb):
    @pl.when(pl.program_id(0) == 0)
    def _():
        st_ref[...] = jnp.zeros_like(st_ref)

    logits = lbl_ref[...]
    ex = jnp.exp(logits - jnp.max(logits, axis=0, keepdims=True))
    lb_all = ex[0:1, :] / jnp.sum(ex, axis=0, keepdims=True)
    C = HG_CHUNK
    row = lax.broadcasted_iota(jnp.int32, (C, C), 0)
    col = lax.broadcasted_iota(jnp.int32, (C, C), 1)
    tril = row >= col
    crow = lax.broadcasted_iota(jnp.int32, (C, HG_D), 0)

    def chunk(c, carry):
        r0 = pl.multiple_of(c * C, C)
        rows = pl.ds(r0, C)
        for b in range(nb):
            for h in range(HG_HEADS):
                sl = slice(HG_D * h, HG_D * (h + 1))
                q, f, iv, g = [hg_ref[b, rows, j * HG_WIDTH + HG_D * h:j * HG_WIDTH + HG_D * (h + 1)] for j in range(4)]
                st = st_ref[b * HG_HEADS + h]
                lb = lb_all[:, sl]
                forget = lb + (1.0 - lb) * jax.nn.sigmoid(f)
                logf = jnp.log(forget)
                key = 1.0 - forget
                query = jax.nn.silu(q) * HG_D ** -0.5
                G = logf
                for d in (1, 2, 4, 8, 16, 32):
                    G = G + jnp.where(crow >= d, pltpu.roll(G, d, 0), 0.0)
                g_ref = G[C // 2 - 1:C // 2, :]
                g_last = G[C - 1:C, :]
                qa = (query * jnp.exp(G - g_ref)).astype(BF16)
                ka = (key * jnp.exp(g_ref - G)).astype(BF16)
                a = jnp.where(tril, _dot_nt(qa, ka), 0.0)
                vb = iv.astype(BF16)
                o = _dot(a.astype(BF16), vb)
                o = o + _dot_nt((query * jnp.exp(G)).astype(BF16), st.astype(BF16))
                kb = (key * jnp.exp(g_last - G)).astype(BF16)
                st_ref[b * HG_HEADS + h] = st * jnp.exp(g_last) + _dot_tn(vb, kb)
                on = _rms(o) * gain_ref[:, sl] * jax.nn.silu(g)
                o_ref[b, rows, sl] = on.astype(o_ref.dtype)
        return carry

    lax.fori_loop(0, cpb, chunk, 0)


def _hgrn(hg3, lb_logits, gain, cpb):
    B, T, W = hg3.shape
    rows = cpb * HG_CHUNK
    return pl.pallas_call(
        functools.partial(_hgrn_kernel, nb=B, cpb=cpb),
        grid=(T // rows,),
        in_specs=[pl.BlockSpec((B, rows, W), lambda i: (0, i, 0)),
                  pl.BlockSpec(lb_logits.shape, lambda i: (0, 0)),
                  pl.BlockSpec(gain.shape, lambda i: (0, 0))],
        out_specs=pl.BlockSpec((B, rows, HG_WIDTH), lambda i: (0, i, 0)),
        out_shape=jax.ShapeDtypeStruct((B, T, HG_WIDTH), BF16),
        scratch_shapes=[pltpu.VMEM((B * HG_HEADS, HG_D, HG_D), F32)],
        compiler_params=_params("arbitrary"),
        name="hgrn2",
    )(hg3, lb_logits, gain)


def _cmp_kernel(xk_ref, xv_ref, w1_ref, w2_ref, pe_ref, kg_ref, kc_ref, vct_ref):
    half = CMP_STRIDE * NSA_DH
    ys = []
    for idx, x_ref in enumerate((xk_ref, xv_ref)):
        x = x_ref[0]
        n = x.shape[0]
        w1 = w1_ref[idx]
        first = _dot(x, w1[:half], HIGHEST)
        second = _dot(x, w1[half:], HIGHEST)
        const = _dot(pe_ref[idx], w1, HIGHEST)[0:1, :]
        y = jax.nn.gelu(first + pltpu.roll(second, n - 1, 0) + const)
        ys.append(_dot(y, w2_ref[idx], HIGHEST))
    kc_ref[0] = (_rms(ys[0]) * kg_ref[0:1, :]).astype(BF16)
    vct_ref[0] = jnp.concatenate(ys, axis=1).T[NSA_DH:].astype(BF16)


def _compress(xk, xv, w1, w2, pe8, kg):
    BG, n, K = xk.shape
    full = lambda a: pl.BlockSpec(a.shape, lambda i: (0,) * a.ndim)
    blk = pl.BlockSpec((1, n, K), lambda i: (i, 0, 0))
    return pl.pallas_call(
        _cmp_kernel,
        grid=(BG,),
        in_specs=[blk, blk, full(w1), full(w2), full(pe8), full(kg)],
        out_specs=[pl.BlockSpec((1, n, NSA_DH), lambda i: (i, 0, 0)), pl.BlockSpec((1, NSA_DH, n), lambda i: (i, 0, 0))],
        out_shape=[jax.ShapeDtypeStruct((BG, n, NSA_DH), BF16), jax.ShapeDtypeStruct((BG, NSA_DH, n), BF16)],
        compiler_params=_params("parallel"),
        name="nsa_compress",
    )(xk, xv, w1, w2, pe8, kg)


def _nsa_kernel(qt_ref, gate_ref, kc_ref, vct_ref, ks_ref, vst_ref, kw_ref, vwt_ref, og_ref,
                o_ref, m_s, l_s, acc_s, sel_s, sa_s, sb_s, pa_s, pb_s, aa_s, ab_s, psum_s, ow_s):
    g = pl.program_id(1)
    i = pl.program_id(2)
    t0 = i * TQ
    width = NSA_R * TQ
    qa = qt_ref[0, 0, 0]
    q = qa[0:NSA_DH]
    slope_row = _alibi_slope_row(g, width)
    hcols = [slice(h * TQ, (h + 1) * TQ) for h in range(NSA_R)]

    ncp = kc_ref.shape[2]
    nn = lax.broadcasted_iota(jnp.int32, (ncp, TQ), 0)
    tt = lax.broadcasted_iota(jnp.int32, (ncp, TQ), 1)
    dist_c = t0 + tt - (nn * CMP_STRIDE + CMP_LEN - 1)
    valid_c = dist_c >= 0
    dist_cf = dist_c.astype(F32)
    s_c = _dot(kc_ref[0, 0], q)
    psum = jnp.zeros((ncp, TQ), F32)
    probs = []
    for h in range(NSA_R):
        s = jnp.where(valid_c, s_c[:, hcols[h]] - slope_row[:, hcols[h]] * dist_cf, NEG)
        m = jnp.max(s, axis=0, keepdims=True)
        m = jnp.where(m > 0.1 * NEG, m, 0.0)
        e = jnp.where(valid_c, jnp.exp(s - m), 0.0)
        p = e / jnp.maximum(jnp.sum(e, axis=0, keepdims=True), 1e-30)
        psum = psum + p
        probs.append(p.astype(BF16))
    o_cmp = _dot(vct_ref[0, 0], jnp.concatenate(probs, axis=1))

    wlen = WIN + TQ
    kw0 = pl.multiple_of(jnp.maximum(t0 - WIN, 0), TQ)
    wkey = lax.broadcasted_iota(jnp.int32, (wlen, TQ), 0)
    wdist = t0 + lax.broadcasted_iota(jnp.int32, (wlen, TQ), 1) - (kw0 + wkey)
    wneg = jnp.where(wdist >= 0, jnp.where(wdist < WIN, 0.0, NEG), NEG)
    sw = _dot(kw_ref[0, 0, pl.ds(kw0, wlen), :], qa) + jnp.concatenate([wneg] * NSA_R, axis=1)
    pw = jnp.exp(sw - jnp.max(sw, axis=0, keepdims=True))
    ow_s[...] = _dot(vwt_ref[0, 0, :, pl.ds(kw0, wlen)], pw.astype(BF16)) / jnp.sum(pw, axis=0, keepdims=True)

    ratio = SLC_LEN // CMP_STRIDE
    weights = np.convolve(np.ones(ratio), np.ones(CMP_LEN // CMP_STRIDE))
    nsb = ncp // ratio
    psum_s[0:ncp, :] = psum
    psum_s[ncp:, :] = jnp.zeros((psum_s.shape[0] - ncp, TQ), F32)
    imp = float(weights[0]) * psum_s[pl.ds(0, nsb, stride=ratio), :]
    for o in range(1, len(weights)):
        imp = imp + float(weights[o]) * psum_s[pl.ds(o, nsb, stride=ratio), :]
    if nsb < 128:
        imp = jnp.concatenate([imp, jnp.zeros((128 - nsb, TQ), F32)], axis=0)
    blk = lax.broadcasted_iota(jnp.int32, (128, TQ), 0).astype(F32)
    tcol = lax.broadcasted_iota(jnp.int32, (128, TQ), 1)
    cur = jnp.right_shift(t0 + tcol, SLC_LEN.bit_length() - 1).astype(F32)
    forced = (blk == 0.0) | (blk == cur) | (blk == cur - 1.0)
    v = jnp.where(forced, -NEG, jnp.where(blk <= cur, imp, NEG))
    sel = jnp.zeros((128, TQ), F32)
    for _ in range(SLC_TOPK):
        m = jnp.max(v, axis=0, keepdims=True)
        idx = jnp.min(jnp.where(v == m, blk, 128.0), axis=0, keepdims=True)
        pick = blk == idx
        sel = jnp.where(pick, 1.0, sel)
        v = jnp.where(pick, REMOVED, v)
    sel_s[...] = sel

    m_s[...] = jnp.full(m_s.shape, NEG, F32)
    l_s[...] = jnp.zeros(l_s.shape, F32)
    acc_s[...] = jnp.zeros(acc_s.shape, F32)
    keyr = lax.broadcasted_iota(jnp.int32, (SLC_CHUNK, TQ), 0)
    tokc = lax.broadcasted_iota(jnp.int32, (SLC_CHUNK, TQ), 1)
    blocks_per_chunk = SLC_CHUNK // SLC_LEN

    last_k0 = ks_ref.shape[2] - SLC_CHUNK

    def key_chunk(c):
        k0 = pl.multiple_of(jnp.minimum(c * SLC_CHUNK, last_k0), SLC_CHUNK)
        return ks_ref[0, 0, pl.ds(k0, SLC_CHUNK), :]

    def value_chunk(c):
        return vst_ref[0, 0, :, pl.ds(pl.multiple_of(c * SLC_CHUNK, SLC_CHUNK), SLC_CHUNK)]

    def softmax_step(s_ref, p_ref, a_ref, c):
        k0 = c * SLC_CHUNK
        selx = jnp.concatenate(
            [jnp.broadcast_to(sel_s[pl.ds(blocks_per_chunk * c + j, 1), :], (SLC_LEN, TQ)) for j in range(blocks_per_chunk)],
            axis=0)
        negm = jnp.where(selx > 0.5, jnp.where(k0 + keyr <= t0 + tokc, 0.0, NEG), NEG)
        s = s_ref[...] + jnp.concatenate([negm] * NSA_R, axis=1)
        m_old = m_s[...]
        m_new = jnp.maximum(m_old, jnp.max(s, axis=0, keepdims=True))
        alpha = jnp.exp(m_old - m_new)
        p = jnp.exp(s - m_new)
        l_s[...] = alpha * l_s[...] + jnp.sum(p, axis=0, keepdims=True)
        m_s[...] = m_new
        a_ref[...] = alpha
        p_ref[...] = p.astype(BF16)

    def accumulate(p_ref, a_ref, c):
        acc_s[...] = a_ref[...] * acc_s[...] + _dot(value_chunk(c), p_ref[...])

    sa_s[...] = _dot(key_chunk(0), qa)
    pb_s[...] = jnp.zeros_like(pb_s)
    ab_s[...] = jnp.ones_like(ab_s)

    def slc_pair(c, carry):
        sb_s[...] = _dot(key_chunk(2 * c + 1), qa)
        softmax_step(sa_s, pa_s, aa_s, 2 * c)
        accumulate(pb_s, ab_s, jnp.maximum(2 * c - 1, 0))
        sa_s[...] = _dot(key_chunk(2 * c + 2), qa)
        softmax_step(sb_s, pb_s, ab_s, 2 * c + 1)
        accumulate(pa_s, aa_s, 2 * c)
        return carry

    n_pairs = lax.shift_right_logical(t0 + TQ + 2 * SLC_CHUNK - 1, SLC_CHUNK.bit_length())
    lax.fori_loop(0, n_pairs, slc_pair, 0)
    accumulate(pb_s, ab_s, 2 * n_pairs - 1)
    o_s = acc_s[...] / l_s[...]

    o_w = ow_s[...]
    gts = gate_ref[...]
    outs = []
    for h in range(NSA_R):
        cols = hcols[h]
        o = (gts[3 * h:3 * h + 1, :] * o_cmp[:, cols] + gts[3 * h + 1:3 * h + 2, :] * o_s[:, cols]
             + gts[3 * h + 2:3 * h + 3, :] * o_w[:, cols])
        outs.append(o * lax.rsqrt(jnp.mean(o * o, axis=0, keepdims=True) + EPS))
    o_ref[...] = (jnp.concatenate(outs, axis=0).T * og_ref[...]).astype(o_ref.dtype)


def _nsa(qt, gates_t, kc, vct, ks, vst, kw, vwt, og):
    B, _, nq, _, width = qt.shape
    T = nq * TQ
    ncp = kc.shape[2]
    rows = pl.BlockSpec((1, 1, T, 2 * NSA_DH), lambda b, g, i: (b, g, 0, 0))
    cols = pl.BlockSpec((1, 1, NSA_DH, T), lambda b, g, i: (b, g, 0, 0))
    ow = NSA_R * NSA_DH
    return pl.pallas_call(
        _nsa_kernel,
        grid=(B, NSA_G, nq),
        in_specs=[pl.BlockSpec((1, 1, 1, 2 * NSA_DH, width), lambda b, g, i: (b, g, i, 0, 0)),
                  pl.BlockSpec((128, TQ), lambda b, g, i: (g, b * nq + i)),
                  pl.BlockSpec((1, 1, ncp, NSA_DH), lambda b, g, i: (b, g, 0, 0)),
                  pl.BlockSpec((1, 1, NSA_DH, ncp), lambda b, g, i: (b, g, 0, 0)),
                  rows, cols, rows, cols,
                  pl.BlockSpec((1, ow), lambda b, g, i: (0, g))],
        out_specs=pl.BlockSpec((TQ, ow), lambda b, g, i: (b * nq + i, g)),
        out_shape=jax.ShapeDtypeStruct((B * T, NSA_G * ow), BF16),
        scratch_shapes=[pltpu.VMEM((1, width), F32), pltpu.VMEM((1, width), F32),
                        pltpu.VMEM((NSA_DH, width), F32), pltpu.VMEM((128, TQ), F32),
                        pltpu.VMEM((SLC_CHUNK, width), F32), pltpu.VMEM((SLC_CHUNK, width), F32),
                        pltpu.VMEM((SLC_CHUNK, width), BF16), pltpu.VMEM((SLC_CHUNK, width), BF16),
                        pltpu.VMEM((1, width), F32), pltpu.VMEM((1, width), F32),
                        pltpu.VMEM((ncp + 8, TQ), F32), pltpu.VMEM((NSA_DH, width), F32)],
        compiler_params=_params("parallel", "parallel", "arbitrary"),
        name="nsa_attn",
    )(qt, gates_t, kc, vct, ks, vst, kw, vwt, og)


def _peer_sel_kernel(x_ref, ohg_ref, onsa_ref, wo1_ref, wo2_ref, fg_ref, wqt_ref, keys_ref,
                     h1_ref, xnt_ref, rank2_ref, e2_ref, lim1_ref, e1_ref, qt_s, r1_s, r2_s, lim_s, top_s):
    h1 = x_ref[...] + _dot(ohg_ref[...], wo1_ref[...]) + _dot(onsa_ref[...], wo2_ref[...])
    h1_ref[...] = h1
    xnt = (_rms(h1) * fg_ref[...]).T.astype(BF16)
    xnt_ref[...] = xnt
    qt_s[...] = _dot(wqt_ref[...], xnt)
    tb = xnt.shape[1]
    K = PEER_TOPK
    rowi = lax.broadcasted_iota(jnp.int32, (PEER_NKEYS, tb), 0).astype(F32)
    k1i = lax.broadcasted_iota(jnp.int32, (K, tb), 0).astype(F32)
    r = lax.broadcasted_iota(jnp.int32, (K + 7 * 8 + 8, tb), 0)
    ci = jnp.where(r < K, r, jnp.where(r < K + 56, K * (1 + jnp.right_shift(r - K, 3)) + jnp.bitwise_and(r - K, 7),
                                       K * (r - (K + 56) + 8))).astype(F32)

    mark = [REMOVED + k * MARK_STEP for k in range(K)]

    def topk_exact(s):
        rank = jnp.full(s.shape, float(PEER_NKEYS), F32)
        tops = []
        for k in range(K):
            m = jnp.max(s, axis=0, keepdims=True)
            idx = jnp.min(jnp.where(s == m, rowi, float(PEER_NKEYS)), axis=0, keepdims=True)
            pick = rowi == idx
            rank = jnp.where(pick, float(k), rank)
            s = jnp.where(pick, REMOVED, s)
            tops.append(m)
        return rank, jnp.concatenate(tops, axis=0)

    def topk_fast(s):
        tops = []
        for k in range(K):
            m = jnp.max(s, axis=0, keepdims=True)
            s = jnp.where(s == m, mark[k], s)
            tops.append(m)
        taken = s < MARK_LIMIT
        rank = jnp.where(taken, jnp.floor((s - mark[0]) * (1.0 / MARK_STEP) + 0.5), float(PEER_NKEYS))
        count = jnp.sum(jnp.where(taken, 1.0, 0.0), axis=0, keepdims=True)
        return rank, jnp.concatenate(tops, axis=0), count

    def candidates(t1, t2):
        return jnp.concatenate([t1[0:1, :] + t2] + [t1[k:k + 1, :] + t2[0:8, :] for k in range(1, 8)]
                               + [t1[8:K, :] + t2[0:1, :]], axis=0)

    def pairs_exact(t1, t2):
        cand = candidates(t1, t2)
        m0 = t1[0:1, :] + t2[0:1, :]
        lim = jnp.zeros((K, tb), F32)
        z = jnp.zeros((1, tb), F32)
        for k in range(K):
            m = jnp.max(cand, axis=0, keepdims=True)
            idx = jnp.min(jnp.where(cand == m, ci, float(K * K)), axis=0, keepdims=True)
            cand = jnp.where(ci == idx, REMOVED, cand)
            lim = lim + jnp.where(k1i == jnp.floor(idx * (1.0 / K)), 1.0, 0.0)
            z = z + jnp.exp(m - m0)
        return lim, z

    def pairs_fast(t1, t2):
        cand = candidates(t1, t2)
        m0 = t1[0:1, :] + t2[0:1, :]
        z = jnp.zeros((1, tb), F32)
        for k in range(K):
            m = jnp.max(cand, axis=0, keepdims=True)
            cand = jnp.where(cand == m, REMOVED, cand)
            z = z + jnp.exp(m - m0)
        taken = jnp.where(cand < MARK_LIMIT, 1.0, 0.0)
        rowsum = lambda a, b: jnp.sum(taken[a:b, :], axis=0, keepdims=True)
        lim = jnp.concatenate([rowsum(0, K)] + [rowsum(K + 8 * j, K + 8 * (j + 1)) for j in range(7)]
                              + [taken[K + 56:, :]], axis=0)
        return lim, z, jnp.sum(taken, axis=0, keepdims=True)

    def store_selection(r1, r2, t1, t2, lim, z):
        r1_s[...] = r1
        r2_s[...] = r2
        lim_s[...] = lim
        top_s[0:1, :] = t1[0:1, :]
        top_s[1:2, :] = t2[0:1, :]
        top_s[2:3, :] = z

    def head(h, carry):
        r = pl.multiple_of(h * 2 * PEER_NKEYS, 2 * PEER_NKEYS)
        s1 = _dot(keys_ref[2 * h], qt_s[pl.ds(r, PEER_NKEYS), :].astype(BF16))
        s2 = _dot(keys_ref[2 * h + 1], qt_s[pl.ds(r + PEER_NKEYS, PEER_NKEYS), :].astype(BF16))
        r1, t1, n1 = topk_fast(s1)
        r2, t2, n2 = topk_fast(s2)
        lim, z, n3 = pairs_fast(t1, t2)
        store_selection(r1, r2, t1, t2, lim, z)
        off_count = jnp.abs(n1 - K) + jnp.abs(n2 - K) + jnp.abs(n3 - K)

        @pl.when(jnp.max(off_count) > 0.5)
        def _():
            r1, t1 = topk_exact(s1)
            r2, t2 = topk_exact(s2)
            lim, z = pairs_exact(t1, t2)
            store_selection(r1, r2, t1, t2, lim, z)

        r1 = r1_s[...]
        lim = lim_s[...]
        rank_bits = r1.astype(jnp.int32)
        level = [lim[k:k + 1, :] for k in range(K)]
        for bit in range(K.bit_length() - 1):
            odd = jnp.bitwise_and(rank_bits, 1 << bit) != 0
            level = [jnp.where(odd, level[2 * i + 1], level[2 * i]) for i in range(len(level) // 2)]
        lim1 = jnp.where(r1 < float(K), level[0], 0.0)
        e1 = jnp.exp(s1 - top_s[0:1, :])
        e2 = jnp.exp(s2 - top_s[1:2, :]) / top_s[2:3, :]
        r2 = r2_s[...]
        for ts in range(tb // 128):
            cols = slice(128 * ts, 128 * (ts + 1))
            rank2_ref[h, ts] = pltpu.bitcast(r2[:, cols].astype(BF16), jnp.uint32)
            lim1_ref[h, ts] = lim1[:, cols]
            e1_ref[h, ts] = e1[:, cols]
            e2_ref[h, ts] = pltpu.bitcast(e2[:, cols].astype(BF16), jnp.uint32)
        return carry

    lax.fori_loop(0, PEER_HEADS, head, 0)


def _peer_select(x2, ohg, onsa, wo1, wo2, fg, wqt, keys, tb):
    N, D = x2.shape
    full = lambda a: pl.BlockSpec(a.shape, lambda i: (0,) * a.ndim)
    sel_spec = pl.BlockSpec((PEER_HEADS, tb // 128, PEER_NKEYS, 128), lambda i: (0, i, 0, 0))
    sel_shape = jax.ShapeDtypeStruct((PEER_HEADS, N // 128, PEER_NKEYS, 128), F32)
    packed_spec = pl.BlockSpec((PEER_HEADS, tb // 128, PEER_NKEYS // 2, 128), lambda i: (0, i, 0, 0))
    packed_shape = jax.ShapeDtypeStruct((PEER_HEADS, N // 128, PEER_NKEYS // 2, 128), jnp.uint32)
    return pl.pallas_call(
        _peer_sel_kernel,
        grid=(N // tb,),
        in_specs=[pl.BlockSpec((tb, D), lambda i: (i, 0)), pl.BlockSpec((tb, ohg.shape[1]), lambda i: (i, 0)),
                  pl.BlockSpec((tb, onsa.shape[1]), lambda i: (i, 0)),
                  full(wo1), full(wo2), full(fg), full(wqt), full(keys)],
        out_specs=[pl.BlockSpec((tb, D), lambda i: (i, 0)), pl.BlockSpec((D, tb), lambda i: (0, i)),
                   packed_spec, packed_spec, sel_spec, sel_spec],
        out_shape=[jax.ShapeDtypeStruct((N, D), F32), jax.ShapeDtypeStruct((D, N), BF16),
                   packed_shape, packed_shape, sel_shape, sel_shape],
        scratch_shapes=[pltpu.VMEM((wqt.shape[0], tb), F32), pltpu.VMEM((PEER_NKEYS, tb), F32),
                        pltpu.VMEM((PEER_NKEYS, tb), F32), pltpu.VMEM((PEER_TOPK, tb), F32), pltpu.VMEM((8, tb), F32)],
        compiler_params=_params("parallel"),
        name="peer_select",
    )(x2, ohg, onsa, wo1, wo2, fg, wqt, keys)


def _peer_dense_kernel(xnt_ref, u_hbm, vt_hbm, rank2_ref, e2_ref, lim1_ref, e1_ref, h1_ref, p_ref,
                       pproj_ref, pgn_ref, pgw_ref, out_ref,
                       acc_ref, at0_s, at1_s, wg0_s, wg1_s, u0_s, u1_s, v0_s, v1_s, xs_s, r2_s, e2_s, lim_s, e1_s,
                       u_sem, v_sem, *, eb, ne):
    e = pl.program_id(1)
    at_s, wg_s, u_s, v_s = (at0_s, at1_s), (wg0_s, wg1_s), (u0_s, u1_s), (v0_s, v1_s)

    def u_copy(block, slot):
        return pltpu.make_async_copy(u_hbm.at[pl.ds(pl.multiple_of(block * eb, eb), eb), :], u_s[slot], u_sem.at[slot])

    def v_copy(block, slot):
        return pltpu.make_async_copy(vt_hbm.at[block], v_s[slot], v_sem.at[slot])

    @pl.when(e == 0)
    def _():
        u_copy(0, 0).start()
        acc_ref[...] = jnp.zeros_like(acc_ref)
        xs_s[...] = xnt_ref[...]
        r2_s[...] = rank2_ref[...]
        e2_s[...] = e2_ref[...]
        lim_s[...] = lim1_ref[...]
        e1_s[...] = e1_ref[...]

    for par in range(2):
        @pl.when((e % 2 == par) & (e < ne))
        def _(par=par):
            u_copy(e, par).wait()

        @pl.when((e % 2 == par) & (e >= 2))
        def _(par=par):
            v_copy(e - 2, par).wait()

        @pl.when((e % 2 == par) & (e + 1 < ne))
        def _(par=par):
            u_copy(e + 1, 1 - par).start()

        @pl.when((e % 2 == par) & (e >= 1) & (e <= ne))
        def _(par=par):
            v_copy(e - 1, 1 - par).start()

    groups = eb // PEER_NKEYS
    half = PEER_NKEYS // 2
    tb = xs_s.shape[1]
    blk_b = e - 1

    def weighting(par, ts):
        cols = slice(128 * ts, 128 * (ts + 1))
        for j0 in range(0, groups, 2):
            wt = [None, None]
            for h in range(PEER_HEADS):
                r2 = pltpu.bitcast(r2_s[h, half * ts:half * (ts + 1), :], BF16)
                w2 = pltpu.bitcast(e2_s[h, half * ts:half * (ts + 1), :], BF16)
                for dj in range(2):
                    i1 = blk_b * groups + j0 + dj
                    lim = jnp.broadcast_to(lim_s[h, ts, pl.ds(i1, 1), :].astype(BF16), (PEER_NKEYS, 128))
                    w1 = jnp.broadcast_to(e1_s[h, ts, pl.ds(i1, 1), :].astype(BF16), (PEER_NKEYS, 128))
                    w = jnp.where(r2 < lim, w2, 0.0) * w1
                    wt[dj] = w if wt[dj] is None else wt[dj] + w
            for dj in range(2):
                rows = slice(PEER_NKEYS * (j0 + dj), PEER_NKEYS * (j0 + dj + 1))
                wg_s[1 - par][rows, cols] = wt[dj] * jax.nn.gelu(at_s[1 - par][rows, cols]).astype(BF16)

    def stages(par, run_a=True, run_b=True, run_c=True):
        nq = tb // 128
        hw = tb // 2
        for q in range(nq):
            cols = slice(hw * (q % 2), hw * (q % 2 + 1))
            if q < nq // 2 and run_a:
                at_s[par][:, cols] = _dot(u_s[par][...], xs_s[:, cols])
            if q >= nq // 2 and run_c:
                acc_ref[:, cols] += _dot(v_s[par][...], wg_s[par][:, cols])
            if run_b:
                weighting(par, q)

    for par in range(2):
        @pl.when((e % 2 == par) & (e >= 2) & (e < ne))
        def _(par=par):
            stages(par)

    @pl.when(e == 0)
    def _():
        stages(0, run_b=False, run_c=False)

    @pl.when(e == 1)
    def _():
        stages(1, run_c=False)

    @pl.when(e == ne)
    def _():
        stages(ne % 2, run_a=False)

    @pl.when(e == ne + 1)
    def _():
        stages((ne + 1) % 2, run_a=False, run_b=False)

    @pl.when(e == ne + 1)
    def _():
        h2 = h1_ref[...] + acc_ref[...].T
        gate = jax.nn.sigmoid(_dot((_rms(h2) * pgn_ref[...]).astype(BF16), pgw_ref[...]))
        out_ref[...] = h2 + _dot(p_ref[...].astype(BF16), pproj_ref[...]) * gate


def _peer_dense(xnt, u, vt, rank2, e2, lim1, e1, h1, p2, pproj, pgn, pgw, tb, eb):
    D, N = xnt.shape
    ne = u.shape[0] // eb
    assert ne >= 2, "the fill/drain steps of the expert-block pipeline assume at least two blocks"
    full = lambda a: pl.BlockSpec(a.shape, lambda t, e: (0,) * a.ndim)
    sel_block =(PEER_HEADS, tb // 128, PEER_NKEYS, 128)
    sel_spec = pl.BlockSpec(sel_block, lambda t, e: (0, t, 0, 0))
    packed_block = (PEER_HEADS, tb // 128 * (PEER_NKEYS // 2), 128)
    packed_spec = pl.BlockSpec(packed_block, lambda t, e: (0, t, 0))
    rank2 = rank2.reshape(PEER_HEADS, -1, 128)
    e2 = e2.reshape(PEER_HEADS, -1, 128)
    any_spec = pl.BlockSpec(memory_space=pl.ANY)
    return pl.pallas_call(
        functools.partial(_peer_dense_kernel, eb=eb, ne=ne),
        grid=(N // tb, ne + 2),
        in_specs=[pl.BlockSpec((D, tb), lambda t, e: (0, t)), any_spec, any_spec,
                  packed_spec, packed_spec, sel_spec, sel_spec,
                  pl.BlockSpec((tb, D), lambda t, e: (t, 0)), pl.BlockSpec((tb, p2.shape[1]), lambda t, e: (t, 0)),
                  full(pproj), full(pgn), full(pgw)],
        out_specs=pl.BlockSpec((tb, D), lambda t, e: (t, 0)),
        out_shape=jax.ShapeDtypeStruct((N, D), F32),
        scratch_shapes=[pltpu.VMEM((D, tb), F32),
                        pltpu.VMEM((eb, tb), F32), pltpu.VMEM((eb, tb), F32),
                        pltpu.VMEM((eb, tb), BF16), pltpu.VMEM((eb, tb), BF16),
                        pltpu.VMEM((eb, D), BF16), pltpu.VMEM((eb, D), BF16),
                        pltpu.VMEM((D, eb), BF16), pltpu.VMEM((D, eb), BF16),
                        pltpu.VMEM((D, tb), BF16),
                        pltpu.VMEM(packed_block, jnp.uint32), pltpu.VMEM(packed_block, jnp.uint32),
                        pltpu.VMEM(sel_block, F32), pltpu.VMEM(sel_block, F32),
                        pltpu.SemaphoreType.DMA((2,)), pltpu.SemaphoreType.DMA((2,))],
        compiler_params=_params("arbitrary", "arbitrary"),
        name="peer_dense",
    )(xnt, u, vt, rank2, e2, lim1, e1, h1, p2, pproj, pgn, pgw)


def _block(n, pref):
    while n % pref:
        pref //= 2
    return pref


def kernel(x, p, mix_norm, w_in, hg_lb_logits, hg_out_norm, nsa_q_norm, nsa_k_norm, cmp_pe, cmp_w1, cmp_w2,
           nsa_out_norm, w_out, ffn_norm, peer_wq, peer_keys, peer_u, peer_v, ple_proj, ple_gate_norm, ple_gate_w):
    B, T, D = x.shape
    N = B * T
    x2 = x.reshape(N, D)
    nsa_w = NSA_HEADS * NSA_DH
    kvw = NSA_G * NSA_DH

    w = w_in[0]
    c0 = 4 * HG_WIDTH
    whg = w[:, :c0].astype(BF16)
    wq = w[:, c0:c0 + nsa_w].astype(BF16)
    wkv = w[:, c0 + nsa_w:c0 + nsa_w + 6 * kvw].astype(BF16)
    wg = w[:, c0 + nsa_w + 6 * kvw:]
    per_g = 3 * NSA_R
    wgate = jnp.zeros((D, 256), F32)
    for g in range(NSA_G):
        wgate = wgate.at[:, 128 * g:128 * g + per_g].set(wg[:, per_g * g:per_g * (g + 1)])
    wgate = wgate.astype(BF16)

    hg, qt, kcr, vcr, ks, vst, kw, vwt, gates_t = _inproj(
        x2, mix_norm, whg, wq, wkv, wgate, nsa_q_norm.reshape(NSA_DH, 1), nsa_k_norm[0], B, T, _block(T, 256))

    o_hg = _hgrn(hg.reshape(B, T, 4 * HG_WIDTH), hg_lb_logits, hg_out_norm, _block(T // HG_CHUNK, 4))

    ncp = T // CMP_STRIDE
    stride_rows = lambda a: a.reshape(B * NSA_G, ncp, CMP_STRIDE * NSA_DH)
    pe8 = jnp.broadcast_to(cmp_pe[0].reshape(2, 1, CMP_LEN * NSA_DH), (2, 8, CMP_LEN * NSA_DH))
    kc, vct = _compress(stride_rows(kcr), stride_rows(vcr), cmp_w1[0], cmp_w2[0], pe8, nsa_k_norm[0])
    kc = kc.reshape(B, NSA_G, ncp, NSA_DH)
    vct = vct.reshape(B, NSA_G, NSA_DH, ncp)

    o_nsa = _nsa(qt, gates_t, kc, vct, ks, vst, kw, vwt, nsa_out_norm)

    wo = w_out[0].astype(BF16)
    wqt = peer_wq[0].T.astype(BF16)
    keys = peer_keys[0].reshape(2 * PEER_HEADS, PEER_NKEYS, -1).astype(BF16)
    h1, xnt, rank2, e2, lim1, e1 = _peer_select(
        x2, o_hg.reshape(N, HG_WIDTH), o_nsa, wo[:HG_WIDTH], wo[HG_WIDTH:], ffn_norm, wqt, keys, _block(N, 512))

    eb = 512
    vt = peer_v[0].reshape(-1, eb, D).transpose(0, 2, 1).astype(BF16)
    out = _peer_dense(xnt, peer_u[0].astype(BF16), vt, rank2, e2, lim1, e1, h1,
                      p[0].reshape(N, -1), ple_proj[0].astype(BF16), ple_gate_norm, ple_gate_w[0].astype(BF16),
                      _block(N, 512), eb)
    return out.reshape(B, T, D)
```

```python
import functools

import numpy as np
import jax
import jax.numpy as jnp
from jax import lax
from jax.experimental import pallas as pl
from jax.experimental.pallas import tpu as pltpu

F32 = jnp.float32
BF16 = jnp.bfloat16
HIGHEST = lax.Precision.HIGHEST
EPS = 1e-6
NEG = -1e30
REMOVED = -3e38
MARK_STEP = 2e36
MARK_LIMIT = -2.5e38

HG_HEADS = 4
HG_D = 128
HG_CHUNK = 64
HG_WIDTH = HG_HEADS * HG_D
NSA_DH = 64
NSA_HEADS = 8
NSA_G = 2
NSA_R = NSA_HEADS // NSA_G
CMP_LEN = 32
CMP_STRIDE = 16
SLC_LEN = 64
SLC_TOPK = 16
WIN = 512
TQ = 128
SLC_CHUNK = 256
PEER_HEADS = 8
PEER_NKEYS = 128
PEER_TOPK = 16
VMEM_LIMIT = 56 * 1024 * 1024


def _dot(a, b, precision=None):
    return lax.dot_general(a, b, (((1,), (0,)), ((), ())), preferred_element_type=F32, precision=precision)


def _dot_nt(a, b):
    return lax.dot_general(a, b, (((1,), (1,)), ((), ())), preferred_element_type=F32)


def _dot_tn(a, b):
    return lax.dot_general(a, b, (((0,), (0,)), ((), ())), preferred_element_type=F32)


def _rms(x):
    return x * lax.rsqrt(jnp.mean(x * x, axis=-1, keepdims=True) + EPS)


def _params(*sem):
    return pltpu.CompilerParams(dimension_semantics=sem, vmem_limit_bytes=VMEM_LIMIT)


def _alibi_slope_row(g, width):
    lane_h = jnp.right_shift(lax.broadcasted_iota(jnp.int32, (1, width), 1), TQ.bit_length() - 1)
    row = jnp.zeros((1, width), F32)
    for h in range(NSA_R):
        row = jnp.where(lane_h == h, jnp.where(g == 0, 2.0 ** -(h + 1), 2.0 ** -(NSA_R + h + 1)).astype(F32), row)
    return row


def _inproj_kernel(x_ref, g_ref, whg_ref, wq_ref, wkv_ref, wgate_ref, qgc_ref, kg_ref,
                   hg_ref, qt_ref, kcr_ref, vcr_ref, ks_ref, vst_ref, kw_ref, vwt_ref, gate_ref, *, nt):
    a = (_rms(x_ref[...]) * g_ref[...]).astype(BF16)
    hg_ref[...] = _dot(a, whg_ref[...])
    tb = a.shape[0]
    nqb = tb // TQ
    width = NSA_R * TQ

    qt = _dot(a, wq_ref[...]).T
    aug_row = lax.broadcasted_iota(jnp.int32, (NSA_DH, width), 0)
    for g in range(NSA_G):
        slope = _alibi_slope_row(g, width)
        aug = jnp.where(aug_row == 0, slope, jnp.where(aug_row == 1, slope * TQ, 0.0)).astype(BF16)
        for j in range(nqb):
            qt_ref[0, g, j, NSA_DH:2 * NSA_DH, :] = aug
        for h in range(NSA_R):
            blk = qt[NSA_DH * (NSA_R * g + h):NSA_DH * (NSA_R * g + h + 1)]
            r = lax.rsqrt(jnp.mean(blk * blk, axis=0, keepdims=True) + EPS)
            qn = (blk * r * qgc_ref[...] * NSA_DH ** -0.5).astype(BF16)
            for j in range(nqb):
                qt_ref[0, g, j, 0:NSA_DH, TQ * h:TQ * (h + 1)] = qn[:, TQ * j:TQ * (j + 1)]

    kv = _dot(a, wkv_ref[...])
    kvt = kv.T
    pos = (pl.program_id(0) % nt) * tb + lax.broadcasted_iota(jnp.int32, (tb, NSA_DH), 0)
    aug_col = lax.broadcasted_iota(jnp.int32, (tb, NSA_DH), 1)
    pos_aug = jnp.where(aug_col == 0, jnp.bitwise_and(pos, TQ - 1),
                        jnp.where(aug_col == 1, jnp.right_shift(pos, TQ.bit_length() - 1), 0)).astype(F32)

    def part(j, g):
        return kv[:, 128 * j + NSA_DH * g:128 * j + NSA_DH * (g + 1)]

    def part_t(j, g):
        return kvt[128 * j + NSA_DH * g:128 * j + NSA_DH * (g + 1)]

    for g in range(NSA_G):
        kcr_ref[0, g] = part(0, g)
        vcr_ref[0, g] = part(1, g)
        ks_ref[0, g] = jnp.concatenate([_rms(part(2, g)) * kg_ref[1:2, :], pos_aug], axis=1).astype(BF16)
        vst_ref[0, g] = part_t(3, g).astype(BF16)
        kw_ref[0, g] = jnp.concatenate([_rms(part(4, g)) * kg_ref[2:3, :], pos_aug], axis=1).astype(BF16)
        vwt_ref[0, g] = part_t(5, g).astype(BF16)
    gate_ref[...] = jax.nn.sigmoid(_dot(a, wgate_ref[...])).T


def _inproj(x2, g, whg, wq, wkv, wgate, qgc, kg, B, T, tb):
    N, D = x2.shape
    nt = T // tb
    nqb = tb // TQ
    full = lambda shape: pl.BlockSpec(shape, lambda i: (0,) * len(shape))
    row_spec = lambda w: pl.BlockSpec((1, NSA_G, tb, w), lambda i: (i // nt, 0, i % nt, 0))
    col_spec = pl.BlockSpec((1, NSA_G, NSA_DH, tb), lambda i: (i // nt, 0, 0, i % nt))
    row_shape = lambda w, dt: jax.ShapeDtypeStruct((B, NSA_G, T, w), dt)
    col_shape = jax.ShapeDtypeStruct((B, NSA_G, NSA_DH, T), BF16)
    return pl.pallas_call(
        functools.partial(_inproj_kernel, nt=nt),
        grid=(N // tb,),
        in_specs=[pl.BlockSpec((tb, D), lambda i: (i, 0)), full(g.shape), full(whg.shape), full(wq.shape),
                  full(wkv.shape), full(wgate.shape), full(qgc.shape), full(kg.shape)],
        out_specs=[pl.BlockSpec((tb, 4 * HG_WIDTH), lambda i: (i, 0)),
                   pl.BlockSpec((1, NSA_G, nqb, 2 * NSA_DH, NSA_R * TQ), lambda i: (i // nt, 0, i % nt, 0, 0)),
                   row_spec(NSA_DH), row_spec(NSA_DH), row_spec(2 * NSA_DH), col_spec, row_spec(2 * NSA_DH), col_spec,
                   pl.BlockSpec((256, tb), lambda i: (0, i))],
        out_shape=[jax.ShapeDtypeStruct((N, 4 * HG_WIDTH), F32),
                   jax.ShapeDtypeStruct((B, NSA_G, T // TQ, 2 * NSA_DH, NSA_R * TQ), BF16),
                   row_shape(NSA_DH, F32), row_shape(NSA_DH, F32), row_shape(2 * NSA_DH, BF16), col_shape,
                   row_shape(2 * NSA_DH, BF16), col_shape,
                   jax.ShapeDtypeStruct((256, N), F32)],
        compiler_params=_params("parallel"),
        name="inproj",
    )(x2, g, whg, wq, wkv, wgate, qgc, kg)


def _hgrn_kernel(hg_ref, lbl_ref, gain_ref, o_ref, st_ref, *, nb, cpb):
    @pl.when(pl.program_id(0) == 0)
    def _():
        st_ref[...] = jnp.zeros_like(st_ref)

    logits = lbl_ref[...]
    ex = jnp.exp(logits - jnp.max(logits, axis=0, keepdims=True))
    lb_all = ex[0:1, :] / jnp.sum(ex, axis=0, keepdims=True)
    C = HG_CHUNK
    row = lax.broadcasted_iota(jnp.int32, (C, C), 0)
    col = lax.broadcasted_iota(jnp.int32, (C, C), 1)
    tril = row >= col
    crow = lax.broadcasted_iota(jnp.int32, (C, HG_D), 0)

    def chunk(c, carry):
        r0 = pl.multiple_of(c * C, C)
        rows = pl.ds(r0, C)
        for b in range(nb):
            for h in range(HG_HEADS):
                sl = slice(HG_D * h, HG_D * (h + 1))
                q, f, iv, g = [hg_ref[b, rows, j * HG_WIDTH + HG_D * h:j * HG_WIDTH + HG_D * (h + 1)] for j in range(4)]
                st = st_ref[b * HG_HEADS + h]
                lb = lb_all[:, sl]
                forget = lb + (1.0 - lb) * jax.nn.sigmoid(f)
                logf = jnp.log(forget)
                key = 1.0 - forget
                query = jax.nn.silu(q) * HG_D ** -0.5
                G = logf
                for d in (1, 2, 4, 8, 16, 32):
                    G = G + jnp.where(crow >= d, pltpu.roll(G, d, 0), 0.0)
                g_ref = G[C // 2 - 1:C // 2, :]
                g_last = G[C - 1:C, :]
                qa = (query * jnp.exp(G - g_ref)).astype(BF16)
                ka = (key * jnp.exp(g_ref - G)).astype(BF16)
                a = jnp.where(tril, _dot_nt(qa, ka), 0.0)
                vb = iv.astype(BF16)
                o = _dot(a.astype(BF16), vb)
                o = o + _dot_nt((query * jnp.exp(G)).astype(BF16), st.astype(BF16))
                kb = (key * jnp.exp(g_last - G)).astype(BF16)
                st_ref[b * HG_HEADS + h] = st * jnp.exp(g_last) + _dot_tn(vb, kb)
                on = _rms(o) * gain_ref[:, sl] * jax.nn.silu(g)
                o_ref[b, rows, sl] = on.astype(o_ref.dtype)
        return carry

    lax.fori_loop(0, cpb, chunk, 0)


def _hgrn(hg3, lb_logits, gain, cpb):
    B, T, W = hg3.shape
    rows = cpb * HG_CHUNK
    return pl.pallas_call(
        functools.partial(_hgrn_kernel, nb=B, cpb=cpb),
        grid=(T // rows,),
        in_specs=[pl.BlockSpec((B, rows, W), lambda i: (0, i, 0)),
                  pl.BlockSpec(lb_logits.shape, lambda i: (0, 0)),
                  pl.BlockSpec(gain.shape, lambda i: (0, 0))],
        out_specs=pl.BlockSpec((B, rows, HG_WIDTH), lambda i: (0, i, 0)),
        out_shape=jax.ShapeDtypeStruct((B, T, HG_WIDTH), BF16),
        scratch_shapes=[pltpu.VMEM((B * HG_HEADS, HG_D, HG_D), F32)],
        compiler_params=_params("arbitrary"),
        name="hgrn2",
    )(hg3, lb_logits, gain)


def _cmp_kernel(xk_ref, xv_ref, w1_ref, w2_ref, pe_ref, kg_ref, kc_ref, vct_ref):
    half = CMP_STRIDE * NSA_DH
    ys = []
    for idx, x_ref in enumerate((xk_ref, xv_ref)):
        x = x_ref[0]
        n = x.shape[0]
        w1 = w1_ref[idx]
        first = _dot(x, w1[:half], HIGHEST)
        second = _dot(x, w1[half:], HIGHEST)
        const = _dot(pe_ref[idx], w1, HIGHEST)[0:1, :]
        y = jax.nn.gelu(first + pltpu.roll(second, n - 1, 0) + const)
        ys.append(_dot(y, w2_ref[idx], HIGHEST))
    kc_ref[0] = (_rms(ys[0]) * kg_ref[0:1, :]).astype(BF16)
    vct_ref[0] = jnp.concatenate(ys, axis=1).T[NSA_DH:].astype(BF16)


def _compress(xk, xv, w1, w2, pe8, kg):
    BG, n, K = xk.shape
    full = lambda a: pl.BlockSpec(a.shape, lambda i: (0,) * a.ndim)
    blk = pl.BlockSpec((1, n, K), lambda i: (i, 0, 0))
    return pl.pallas_call(
        _cmp_kernel,
        grid=(BG,),
        in_specs=[blk, blk, full(w1), full(w2), full(pe8), full(kg)],
        out_specs=[pl.BlockSpec((1, n, NSA_DH), lambda i: (i, 0, 0)), pl.BlockSpec((1, NSA_DH, n), lambda i: (i, 0, 0))],
        out_shape=[jax.ShapeDtypeStruct((BG, n, NSA_DH), BF16), jax.ShapeDtypeStruct((BG, NSA_DH, n), BF16)],
        compiler_params=_params("parallel"),
        name="nsa_compress",
    )(xk, xv, w1, w2, pe8, kg)


def _nsa_kernel(qt_ref, gate_ref, kc_ref, vct_ref, ks_ref, vst_ref, kw_ref, vwt_ref, og_ref,
                o_ref, m_s, l_s, acc_s, sel_s, sa_s, sb_s, pa_s, pb_s, aa_s, ab_s, psum_s, ow_s):
    g = pl.program_id(1)
    i = pl.program_id(2)
    t0 = i * TQ
    width = NSA_R * TQ
    qa = qt_ref[0, 0, 0]
    q = qa[0:NSA_DH]
    slope_row = _alibi_slope_row(g, width)
    hcols = [slice(h * TQ, (h + 1) * TQ) for h in range(NSA_R)]

    ncp = kc_ref.shape[2]
    nn = lax.broadcasted_iota(jnp.int32, (ncp, TQ), 0)
    tt = lax.broadcasted_iota(jnp.int32, (ncp, TQ), 1)
    dist_c = t0 + tt - (nn * CMP_STRIDE + CMP_LEN - 1)
    valid_c = dist_c >= 0
    dist_cf = dist_c.astype(F32)
    s_c = _dot(kc_ref[0, 0], q)
    psum = jnp.zeros((ncp, TQ), F32)
    probs = []
    for h in range(NSA_R):
        s = jnp.where(valid_c, s_c[:, hcols[h]] - slope_row[:, hcols[h]] * dist_cf, NEG)
        m = jnp.max(s, axis=0, keepdims=True)
        m = jnp.where(m > 0.1 * NEG, m, 0.0)
        e = jnp.where(valid_c, jnp.exp(s - m), 0.0)
        p = e / jnp.maximum(jnp.sum(e, axis=0, keepdims=True), 1e-30)
        psum = psum + p
        probs.append(p.astype(BF16))
    o_cmp = _dot(vct_ref[0, 0], jnp.concatenate(probs, axis=1))

    wlen = WIN + TQ
    kw0 = pl.multiple_of(jnp.maximum(t0 - WIN, 0), TQ)
    wkey = lax.broadcasted_iota(jnp.int32, (wlen, TQ), 0)
    wdist = t0 + lax.broadcasted_iota(jnp.int32, (wlen, TQ), 1) - (kw0 + wkey)
    wneg = jnp.where(wdist >= 0, jnp.where(wdist < WIN, 0.0, NEG), NEG)
    sw = _dot(kw_ref[0, 0, pl.ds(kw0, wlen), :], qa) + jnp.concatenate([wneg] * NSA_R, axis=1)
    pw = jnp.exp(sw - jnp.max(sw, axis=0, keepdims=True))
    ow_s[...] = _dot(vwt_ref[0, 0, :, pl.ds(kw0, wlen)], pw.astype(BF16)) / jnp.sum(pw, axis=0, keepdims=True)

    ratio = SLC_LEN // CMP_STRIDE
    weights = np.convolve(np.ones(ratio), np.ones(CMP_LEN // CMP_STRIDE))
    nsb = ncp // ratio
    psum_s[0:ncp, :] = psum
    psum_s[ncp:, :] = jnp.zeros((psum_s.shape[0] - ncp, TQ), F32)
    imp = float(weights[0]) * psum_s[pl.ds(0, nsb, stride=ratio), :]
    for o in range(1, len(weights)):
        imp = imp + float(weights[o]) * psum_s[pl.ds(o, nsb, stride=ratio), :]
    if nsb < 128:
        imp = jnp.concatenate([imp, jnp.zeros((128 - nsb, TQ), F32)], axis=0)
    blk = lax.broadcasted_iota(jnp.int32, (128, TQ), 0).astype(F32)
    tcol = lax.broadcasted_iota(jnp.int32, (128, TQ), 1)
    cur = jnp.right_shift(t0 + tcol, SLC_LEN.bit_length() - 1).astype(F32)
    forced = (blk == 0.0) | (blk == cur) | (blk == cur - 1.0)
    v = jnp.where(forced, -NEG, jnp.where(blk <= cur, imp, NEG))
    sel = jnp.zeros((128, TQ), F32)
    for _ in range(SLC_TOPK):
        m = jnp.max(v, axis=0, keepdims=True)
        idx = jnp.min(jnp.where(v == m, blk, 128.0), axis=0, keepdims=True)
        pick = blk == idx
        sel = jnp.where(pick, 1.0, sel)
        v = jnp.where(pick, REMOVED, v)
    sel_s[...] = sel

    m_s[...] = jnp.full(m_s.shape, NEG, F32)
    l_s[...] = jnp.zeros(l_s.shape, F32)
    acc_s[...] = jnp.zeros(acc_s.shape, F32)
    keyr = lax.broadcasted_iota(jnp.int32, (SLC_CHUNK, TQ), 0)
    tokc = lax.broadcasted_iota(jnp.int32, (SLC_CHUNK, TQ), 1)
    blocks_per_chunk = SLC_CHUNK // SLC_LEN

    last_k0 = ks_ref.shape[2] - SLC_CHUNK

    def key_chunk(c):
        k0 = pl.multiple_of(jnp.minimum(c * SLC_CHUNK, last_k0), SLC_CHUNK)
        return ks_ref[0, 0, pl.ds(k0, SLC_CHUNK), :]

    def value_chunk(c):
        return vst_ref[0, 0, :, pl.ds(pl.multiple_of(c * SLC_CHUNK, SLC_CHUNK), SLC_CHUNK)]

    def softmax_step(s_ref, p_ref, a_ref, c):
        k0 = c * SLC_CHUNK
        selx = jnp.concatenate(
            [jnp.broadcast_to(sel_s[pl.ds(blocks_per_chunk * c + j, 1), :], (SLC_LEN, TQ)) for j in range(blocks_per_chunk)],
            axis=0)
        negm = jnp.where(selx > 0.5, jnp.where(k0 + keyr <= t0 + tokc, 0.0, NEG), NEG)
        s = s_ref[...] + jnp.concatenate([negm] * NSA_R, axis=1)
        m_old = m_s[...]
        m_new = jnp.maximum(m_old, jnp.max(s, axis=0, keepdims=True))
        alpha = jnp.exp(m_old - m_new)
        p = jnp.exp(s - m_new)
        l_s[...] = alpha * l_s[...] + jnp.sum(p, axis=0, keepdims=True)
        m_s[...] = m_new
        a_ref[...] = alpha
        p_ref[...] = p.astype(BF16)

    def accumulate(p_ref, a_ref, c):
        acc_s[...] = a_ref[...] * acc_s[...] + _dot(value_chunk(c), p_ref[...])

    sa_s[...] = _dot(key_chunk(0), qa)
    pb_s[...] = jnp.zeros_like(pb_s)
    ab_s[...] = jnp.ones_like(ab_s)

    def slc_pair(c, carry):
        sb_s[...] = _dot(key_chunk(2 * c + 1), qa)
        softmax_step(sa_s, pa_s, aa_s, 2 * c)
        accumulate(pb_s, ab_s, jnp.maximum(2 * c - 1, 0))
        sa_s[...] = _dot(key_chunk(2 * c + 2), qa)
        softmax_step(sb_s, pb_s, ab_s, 2 * c + 1)
        accumulate(pa_s, aa_s, 2 * c)
        return carry

    n_pairs = lax.shift_right_logical(t0 + TQ + 2 * SLC_CHUNK - 1, SLC_CHUNK.bit_length())
    lax.fori_loop(0, n_pairs, slc_pair, 0)
    accumulate(pb_s, ab_s, 2 * n_pairs - 1)
    o_s = acc_s[...] / l_s[...]

    o_w = ow_s[...]
    gts = gate_ref[...]
    outs = []
    for h in range(NSA_R):
        cols = hcols[h]
        o = (gts[3 * h:3 * h + 1, :] * o_cmp[:, cols] + gts[3 * h + 1:3 * h + 2, :] * o_s[:, cols]
             + gts[3 * h + 2:3 * h + 3, :] * o_w[:, cols])
        outs.append(o * lax.rsqrt(jnp.mean(o * o, axis=0, keepdims=True) + EPS))
    o_ref[...] = (jnp.concatenate(outs, axis=0).T * og_ref[...]).astype(o_ref.dtype)


def _nsa(qt, gates_t, kc, vct, ks, vst, kw, vwt, og):
    B, _, nq, _, width = qt.shape
    T = nq * TQ
    ncp = kc.shape[2]
    rows = pl.BlockSpec((1, 1, T, 2 * NSA_DH), lambda b, g, i: (b, g, 0, 0))
    cols = pl.BlockSpec((1, 1, NSA_DH, T), lambda b, g, i: (b, g, 0, 0))
    ow = NSA_R * NSA_DH
    return pl.pallas_call(
        _nsa_kernel,
        grid=(B, NSA_G, nq),
        in_specs=[pl.BlockSpec((1, 1, 1, 2 * NSA_DH, width), lambda b, g, i: (b, g, i, 0, 0)),
                  pl.BlockSpec((128, TQ), lambda b, g, i: (g, b * nq + i)),
                  pl.BlockSpec((1, 1, ncp, NSA_DH), lambda b, g, i: (b, g, 0, 0)),
                  pl.BlockSpec((1, 1, NSA_DH, ncp), lambda b, g, i: (b, g, 0, 0)),
                  rows, cols, rows, cols,
                  pl.BlockSpec((1, ow), lambda b, g, i: (0, g))],
        out_specs=pl.BlockSpec((TQ, ow), lambda b, g, i: (b * nq + i, g)),
        out_shape=jax.ShapeDtypeStruct((B * T, NSA_G * ow), BF16),
        scratch_shapes=[pltpu.VMEM((1, width), F32), pltpu.VMEM((1, width), F32),
                        pltpu.VMEM((NSA_DH, width), F32), pltpu.VMEM((128, TQ), F32),
                        pltpu.VMEM((SLC_CHUNK, width), F32), pltpu.VMEM((SLC_CHUNK, width), F32),
                        pltpu.VMEM((SLC_CHUNK, width), BF16), pltpu.VMEM((SLC_CHUNK, width), BF16),
                        pltpu.VMEM((1, width), F32), pltpu.VMEM((1, width), F32),
                        pltpu.VMEM((ncp + 8, TQ), F32), pltpu.VMEM((NSA_DH, width), F32)],
        compiler_params=_params("parallel", "parallel", "arbitrary"),
        name="nsa_attn",
    )(qt, gates_t, kc, vct, ks, vst, kw, vwt, og)


def _peer_sel_kernel(x_ref, ohg_ref, onsa_ref, wo1_ref, wo2_ref, fg_ref, wqt_ref, keys_ref,
                     h1_ref, xnt_ref, rank2_ref, e2_ref, lim1_ref, e1_ref, qt_s, r1_s, r2_s, lim_s, top_s):
    h1 = x_ref[...] + _dot(ohg_ref[...], wo1_ref[...]) + _dot(onsa_ref[...], wo2_ref[...])
    h1_ref[...] = h1
    xnt = (_rms(h1) * fg_ref[...]).T.astype(BF16)
    xnt_ref[...] = xnt
    qt_s[...] = _dot(wqt_ref[...], xnt)
    tb = xnt.shape[1]
    K = PEER_TOPK
    rowi = lax.broadcasted_iota(jnp.int32, (PEER_NKEYS, tb), 0).astype(F32)
    k1i = lax.broadcasted_iota(jnp.int32, (K, tb), 0).astype(F32)
    r = lax.broadcasted_iota(jnp.int32, (K + 7 * 8 + 8, tb), 0)
    ci = jnp.where(r < K, r, jnp.where(r < K + 56, K * (1 + jnp.right_shift(r - K, 3)) + jnp.bitwise_and(r - K, 7),
                                       K * (r - (K + 56) + 8))).astype(F32)

    mark = [REMOVED + k * MARK_STEP for k in range(K)]

    def topk_exact(s):
        rank = jnp.full(s.shape, float(PEER_NKEYS), F32)
        tops = []
        for k in range(K):
            m = jnp.max(s, axis=0, keepdims=True)
            idx = jnp.min(jnp.where(s == m, rowi, float(PEER_NKEYS)), axis=0, keepdims=True)
            pick = rowi == idx
            rank = jnp.where(pick, float(k), rank)
            s = jnp.where(pick, REMOVED, s)
            tops.append(m)
        return rank, jnp.concatenate(tops, axis=0)

    def topk_fast(s):
        tops = []
        for k in range(K):
            m = jnp.max(s, axis=0, keepdims=True)
            s = jnp.where(s == m, mark[k], s)
            tops.append(m)
        taken = s < MARK_LIMIT
        rank = jnp.where(taken, jnp.floor((s - mark[0]) * (1.0 / MARK_STEP) + 0.5), float(PEER_NKEYS))
        count = jnp.sum(jnp.where(taken, 1.0, 0.0), axis=0, keepdims=True)
        return rank, jnp.concatenate(tops, axis=0), count

    def candidates(t1, t2):
        return jnp.concatenate([t1[0:1, :] + t2] + [t1[k:k + 1, :] + t2[0:8, :] for k in range(1, 8)]
                               + [t1[8:K, :] + t2[0:1, :]], axis=0)

    def pairs_exact(t1, t2):
        cand = candidates(t1, t2)
        m0 = t1[0:1, :] + t2[0:1, :]
        lim = jnp.zeros((K, tb), F32)
        z = jnp.zeros((1, tb), F32)
        for k in range(K):
            m = jnp.max(cand, axis=0, keepdims=True)
            idx = jnp.min(jnp.where(cand == m, ci, float(K * K)), axis=0, keepdims=True)
            cand = jnp.where(ci == idx, REMOVED, cand)
            lim = lim + jnp.where(k1i == jnp.floor(idx * (1.0 / K)), 1.0, 0.0)
            z = z + jnp.exp(m - m0)
        return lim, z

    def pairs_fast(t1, t2):
        cand = candidates(t1, t2)
        m0 = t1[0:1, :] + t2[0:1, :]
        z = jnp.zeros((1, tb), F32)
        for k in range(K):
            m = jnp.max(cand, axis=0, keepdims=True)
            cand = jnp.where(cand == m, REMOVED, cand)
            z = z + jnp.exp(m - m0)
        taken = jnp.where(cand < MARK_LIMIT, 1.0, 0.0)
        rowsum = lambda a, b: jnp.sum(taken[a:b, :], axis=0, keepdims=True)
        lim = jnp.concatenate([rowsum(0, K)] + [rowsum(K + 8 * j, K + 8 * (j + 1)) for j in range(7)]
                              + [taken[K + 56:, :]], axis=0)
        return lim, z, jnp.sum(taken, axis=0, keepdims=True)

    def store_selection(r1, r2, t1, t2, lim, z):
        r1_s[...] = r1
        r2_s[...] = r2
        lim_s[...] = lim
        top_s[0:1, :] = t1[0:1, :]
        top_s[1:2, :] = t2[0:1, :]
        top_s[2:3, :] = z

    def head(h, carry):
        r = pl.multiple_of(h * 2 * PEER_NKEYS, 2 * PEER_NKEYS)
        s1 = _dot(keys_ref[2 * h], qt_s[pl.ds(r, PEER_NKEYS), :].astype(BF16))
        s2 = _dot(keys_ref[2 * h + 1], qt_s[pl.ds(r + PEER_NKEYS, PEER_NKEYS), :].astype(BF16))
        r1, t1, n1 = topk_fast(s1)
        r2, t2, n2 = topk_fast(s2)
        lim, z, n3 = pairs_fast(t1, t2)
        store_selection(r1, r2, t1, t2, lim, z)
        off_count = jnp.abs(n1 - K) + jnp.abs(n2 - K) + jnp.abs(n3 - K)

        @pl.when(jnp.max(off_count) > 0.5)
        def _():
            r1, t1 = topk_exact(s1)
            r2, t2 = topk_exact(s2)
            lim, z = pairs_exact(t1, t2)
            store_selection(r1, r2, t1, t2, lim, z)

        r1 = r1_s[...]
        lim = lim_s[...]
        rank_bits = r1.astype(jnp.int32)
        level = [lim[k:k + 1, :] for k in range(K)]
        for bit in range(K.bit_length() - 1):
            odd = jnp.bitwise_and(rank_bits, 1 << bit) != 0
            level = [jnp.where(odd, level[2 * i + 1], level[2 * i]) for i in range(len(level) // 2)]
        lim1 = jnp.where(r1 < float(K), level[0], 0.0)
        e1 = jnp.exp(s1 - top_s[0:1, :])
        e2 = jnp.exp(s2 - top_s[1:2, :]) / top_s[2:3, :]
        r2 = r2_s[...]
        for ts in range(tb // 128):
            cols = slice(128 * ts, 128 * (ts + 1))
            rank2_ref[h, ts] = pltpu.bitcast(r2[:, cols].astype(BF16), jnp.uint32)
            lim1_ref[h, ts] = lim1[:, cols]
            e1_ref[h, ts] = e1[:, cols]
            e2_ref[h, ts] = pltpu.bitcast(e2[:, cols].astype(BF16), jnp.uint32)
        return carry

    lax.fori_loop(0, PEER_HEADS, head, 0)


def _peer_select(x2, ohg, onsa, wo1, wo2, fg, wqt, keys, tb):
    N, D = x2.shape
    full = lambda a: pl.BlockSpec(a.shape, lambda i: (0,) * a.ndim)
    sel_spec = pl.BlockSpec((PEER_HEADS, tb // 128, PEER_NKEYS, 128), lambda i: (0, i, 0, 0))
    sel_shape = jax.ShapeDtypeStruct((PEER_HEADS, N // 128, PEER_NKEYS, 128), F32)
    packed_spec = pl.BlockSpec((PEER_HEADS, tb // 128, PEER_NKEYS // 2, 128), lambda i: (0, i, 0, 0))
    packed_shape = jax.ShapeDtypeStruct((PEER_HEADS, N // 128, PEER_NKEYS // 2, 128), jnp.uint32)
    return pl.pallas_call(
        _peer_sel_kernel,
        grid=(N // tb,),
        in_specs=[pl.BlockSpec((tb, D), lambda i: (i, 0)), pl.BlockSpec((tb, ohg.shape[1]), lambda i: (i, 0)),
                  pl.BlockSpec((tb, onsa.shape[1]), lambda i: (i, 0)),
                  full(wo1), full(wo2), full(fg), full(wqt), full(keys)],
        out_specs=[pl.BlockSpec((tb, D), lambda i: (i, 0)), pl.BlockSpec((D, tb), lambda i: (0, i)),
                   packed_spec, packed_spec, sel_spec, sel_spec],
        out_shape=[jax.ShapeDtypeStruct((N, D), F32), jax.ShapeDtypeStruct((D, N), BF16),
                   packed_shape, packed_shape, sel_shape, sel_shape],
        scratch_shapes=[pltpu.VMEM((wqt.shape[0], tb), F32), pltpu.VMEM((PEER_NKEYS, tb), F32),
                        pltpu.VMEM((PEER_NKEYS, tb), F32), pltpu.VMEM((PEER_TOPK, tb), F32), pltpu.VMEM((8, tb), F32)],
        compiler_params=_params("parallel"),
        name="peer_select",
    )(x2, ohg, onsa, wo1, wo2, fg, wqt, keys)


def _peer_dense_kernel(xnt_ref, u_hbm, vt_hbm, rank2_ref, e2_ref, lim1_ref, e1_ref, h1_ref, p_ref,
                       pproj_ref, pgn_ref, pgw_ref, out_ref,
                       acc_ref, at0_s, at1_s, wg0_s, wg1_s, u0_s, u1_s, v0_s, v1_s, xs_s, r2_s, e2_s, lim_s, e1_s,
                       u_sem, v_sem, *, eb, ne):
    e = pl.program_id(1)
    at_s, wg_s, u_s, v_s = (at0_s, at1_s), (wg0_s, wg1_s), (u0_s, u1_s), (v0_s, v1_s)

    def u_copy(block, slot):
        return pltpu.make_async_copy(u_hbm.at[pl.ds(pl.multiple_of(block * eb, eb), eb), :], u_s[slot], u_sem.at[slot])

    def v_copy(block, slot):
        return pltpu.make_async_copy(vt_hbm.at[block], v_s[slot], v_sem.at[slot])

    @pl.when(e == 0)
    def _():
        u_copy(0, 0).start()
        acc_ref[...] = jnp.zeros_like(acc_ref)
        xs_s[...] = xnt_ref[...]
        r2_s[...] = rank2_ref[...]
        e2_s[...] = e2_ref[...]
        lim_s[...] = lim1_ref[...]
        e1_s[...] = e1_ref[...]

    for par in range(2):
        @pl.when((e % 2 == par) & (e < ne))
        def _(par=par):
            u_copy(e, par).wait()

        @pl.when((e % 2 == par) & (e >= 2))
        def _(par=par):
            v_copy(e - 2, par).wait()

        @pl.when((e % 2 == par) & (e + 1 < ne))
        def _(par=par):
            u_copy(e + 1, 1 - par).start()

        @pl.when((e % 2 == par) & (e >= 1) & (e <= ne))
        def _(par=par):
            v_copy(e - 1, 1 - par).start()

    groups = eb // PEER_NKEYS
    half = PEER_NKEYS // 2
    tb = xs_s.shape[1]
    blk_b = e - 1

    def weighting(par, ts):
        cols = slice(128 * ts, 128 * (ts + 1))
        hk = PEER_NKEYS // 2
        for sub in range(2):
            packed = slice(half * ts + (hk // 2) * sub, half * ts + (hk // 2) * (sub + 1))
            wt = [None] * groups
            for h in range(PEER_HEADS):
                r2 = pltpu.bitcast(r2_s[h, packed, :], BF16)
                w2 = pltpu.bitcast(e2_s[h, packed, :], BF16)
                for j in range(groups):
                    i1 = blk_b * groups + j
                    lim = jnp.broadcast_to(lim_s[h, ts, pl.ds(i1, 1), :].astype(BF16), (hk, 128))
                    w1 = jnp.broadcast_to(e1_s[h, ts, pl.ds(i1, 1), :].astype(BF16), (hk, 128))
                    w = jnp.where(r2 < lim, w2, 0.0) * w1
                    wt[j] = w if wt[j] is None else wt[j] + w
            for j in range(groups):
                rows = slice(PEER_NKEYS * j + hk * sub, PEER_NKEYS * j + hk * (sub + 1))
                wg_s[1 - par][rows, cols] = wt[j] * jax.nn.gelu(at_s[1 - par][rows, cols]).astype(BF16)

    def stages(par, run_a=True, run_b=True, run_c=True):
        nq = tb // 128
        hw = tb // 2
        for q in range(nq):
            cols = slice(hw * (q % 2), hw * (q % 2 + 1))
            if q < nq // 2 and run_a:
                at_s[par][:, cols] = _dot(u_s[par][...], xs_s[:, cols])
            if q >= nq // 2 and run_c:
                acc_ref[:, cols] += _dot(v_s[par][...], wg_s[par][:, cols])
            if run_b:
                weighting(par, q)

    for par in range(2):
        @pl.when((e % 2 == par) & (e >= 2) & (e < ne))
        def _(par=par):
            stages(par)

    @pl.when(e == 0)
    def _():
        stages(0, run_b=False, run_c=False)

    @pl.when(e == 1)
    def _():
        stages(1, run_c=False)

    @pl.when(e == ne)
    def _():
        stages(ne % 2, run_a=False)

    @pl.when(e == ne + 1)
    def _():
        stages((ne + 1) % 2, run_a=False, run_b=False)

    @pl.when(e == ne + 1)
    def _():
        h2 = h1_ref[...] + acc_ref[...].T
        gate = jax.nn.sigmoid(_dot((_rms(h2) * pgn_ref[...]).astype(BF16), pgw_ref[...]))
        out_ref[...] = h2 + _dot(p_ref[...].astype(BF16), pproj_ref[...]) * gate


def _peer_dense(xnt, u, vt, rank2, e2, lim1, e1, h1, p2, pproj, pgn, pgw, tb, eb):
    D, N = xnt.shape
    ne = u.shape[0] // eb
    assert ne >= 2, "the fill/drain steps of the expert-block pipeline assume at least two blocks"
    full = lambda a: pl.BlockSpec(a.shape, lambda t, e: (0,) * a.ndim)
    sel_block =(PEER_HEADS, tb // 128, PEER_NKEYS, 128)
    sel_spec = pl.BlockSpec(sel_block, lambda t, e: (0, t, 0, 0))
    packed_block = (PEER_HEADS, tb // 128 * (PEER_NKEYS // 2), 128)
    packed_spec = pl.BlockSpec(packed_block, lambda t, e: (0, t, 0))
    rank2 = rank2.reshape(PEER_HEADS, -1, 128)
    e2 = e2.reshape(PEER_HEADS, -1, 128)
    any_spec = pl.BlockSpec(memory_space=pl.ANY)
    return pl.pallas_call(
        functools.partial(_peer_dense_kernel, eb=eb, ne=ne),
        grid=(N // tb, ne + 2),
        in_specs=[pl.BlockSpec((D, tb), lambda t, e: (0, t)), any_spec, any_spec,
                  packed_spec, packed_spec, sel_spec, sel_spec,
                  pl.BlockSpec((tb, D), lambda t, e: (t, 0)), pl.BlockSpec((tb, p2.shape[1]), lambda t, e: (t, 0)),
                  full(pproj), full(pgn), full(pgw)],
        out_specs=pl.BlockSpec((tb, D), lambda t, e: (t, 0)),
        out_shape=jax.ShapeDtypeStruct((N, D), F32),
        scratch_shapes=[pltpu.VMEM((D, tb), F32),
                        pltpu.VMEM((eb, tb), F32), pltpu.VMEM((eb, tb), F32),
                        pltpu.VMEM((eb, tb), BF16), pltpu.VMEM((eb, tb), BF16),
                        pltpu.VMEM((eb, D), BF16), pltpu.VMEM((eb, D), BF16),
                        pltpu.VMEM((D, eb), BF16), pltpu.VMEM((D, eb), BF16),
                        pltpu.VMEM((D, tb), BF16),
                        pltpu.VMEM(packed_block, jnp.uint32), pltpu.VMEM(packed_block, jnp.uint32),
                        pltpu.VMEM(sel_block, F32), pltpu.VMEM(sel_block, F32),
                        pltpu.SemaphoreType.DMA((2,)), pltpu.SemaphoreType.DMA((2,))],
        compiler_params=_params("arbitrary", "arbitrary"),
        name="peer_dense",
    )(xnt, u, vt, rank2, e2, lim1, e1, h1, p2, pproj, pgn, pgw)


def _block(n, pref):
    while n % pref:
        pref //= 2
    return pref


def kernel(x, p, mix_norm, w_in, hg_lb_logits, hg_out_norm, nsa_q_norm, nsa_k_norm, cmp_pe, cmp_w1, cmp_w2,
           nsa_out_norm, w_out, ffn_norm, peer_wq, peer_keys, peer_u, peer_v, ple_proj, ple_gate_norm, ple_gate_w):
    B, T, D = x.shape
    N = B * T
    x2 = x.reshape(N, D)
    nsa_w = NSA_HEADS * NSA_DH
    kvw = NSA_G * NSA_DH

    w = w_in[0]
    c0 = 4 * HG_WIDTH
    whg = w[:, :c0].astype(BF16)
    wq = w[:, c0:c0 + nsa_w].astype(BF16)
    wkv = w[:, c0 + nsa_w:c0 + nsa_w + 6 * kvw].astype(BF16)
    wg = w[:, c0 + nsa_w + 6 * kvw:]
    per_g = 3 * NSA_R
    wgate = jnp.zeros((D, 256), F32)
    for g in range(NSA_G):
        wgate = wgate.at[:, 128 * g:128 * g + per_g].set(wg[:, per_g * g:per_g * (g + 1)])
    wgate = wgate.astype(BF16)

    hg, qt, kcr, vcr, ks, vst, kw, vwt, gates_t = _inproj(
        x2, mix_norm, whg, wq, wkv, wgate, nsa_q_norm.reshape(NSA_DH, 1), nsa_k_norm[0], B, T, _block(T, 256))

    o_hg = _hgrn(hg.reshape(B, T, 4 * HG_WIDTH), hg_lb_logits, hg_out_norm, _block(T // HG_CHUNK, 4))

    ncp = T // CMP_STRIDE
    stride_rows = lambda a: a.reshape(B * NSA_G, ncp, CMP_STRIDE * NSA_DH)
    pe8 = jnp.broadcast_to(cmp_pe[0].reshape(2, 1, CMP_LEN * NSA_DH), (2, 8, CMP_LEN * NSA_DH))
    kc, vct = _compress(stride_rows(kcr), stride_rows(vcr), cmp_w1[0], cmp_w2[0], pe8, nsa_k_norm[0])
    kc = kc.reshape(B, NSA_G, ncp, NSA_DH)
    vct = vct.reshape(B, NSA_G, NSA_DH, ncp)

    o_nsa = _nsa(qt, gates_t, kc, vct, ks, vst, kw, vwt, nsa_out_norm)

    wo = w_out[0].astype(BF16)
    wqt = peer_wq[0].T.astype(BF16)
    keys = peer_keys[0].reshape(2 * PEER_HEADS, PEER_NKEYS, -1).astype(BF16)
    h1, xnt, rank2, e2, lim1, e1 = _peer_select(
        x2, o_hg.reshape(N, HG_WIDTH), o_nsa, wo[:HG_WIDTH], wo[HG_WIDTH:], ffn_norm, wqt, keys, _block(N, 512))

    eb = 512
    vt = peer_v[0].reshape(-1, eb, D).transpose(0, 2, 1).astype(BF16)
    out = _peer_dense(xnt, peer_u[0].astype(BF16), vt, rank2, e2, lim1, e1, h1,
                      p[0].reshape(N, -1), ple_proj[0].astype(BF16), ple_gate_norm, ple_gate_w[0].astype(BF16),
                      _block(N, 512), eb)
    return out.reshape(B, T, D)
```
